```python
import math
import jax, jax.numpy as jnp
from jax import lax
import numpy as np

D_MODEL = 2048
BATCH = 4
SEQ = 8192
DEPTH = 1

ATT_HEADS = 8
ATT_QK_DIM = 64
ATT_V_DIM = 2 * ATT_QK_DIM
D_ATT = ATT_HEADS * ATT_V_DIM
D_ATT_QK = ATT_HEADS * 2 * ATT_QK_DIM
Q_BLOCK = 128

D_SSM = D_MODEL // 2
SSM_GROUP = 16
SSM_GROUPS = D_SSM // SSM_GROUP
SSM_STATE = 64
DT_MIN = 1e-3
DT_MAX = 1e-1

IN_SIZES = (D_ATT_QK, D_ATT_QK, D_ATT, D_ATT, D_SSM, D_SSM, D_MODEL, D_MODEL)
N_IN = sum(IN_SIZES)
RMS_EPS = 1e-6

kernel_name = "hybrid_diffattn_s5_gated_block"


def rms_norm(x, gain):
    xf = x.astype(jnp.float32)
    y = xf * lax.rsqrt(jnp.mean(xf * xf, axis=-1, keepdims=True) + RMS_EPS)
    return (y * gain.astype(jnp.float32)).astype(x.dtype)


def lambda_init_fn(layer_idx):
    return 0.8 - 0.6 * math.exp(-0.3 * layer_idx)


def diff_attention(q, k, v, lam):
    bsz, seq = q.shape[0], q.shape[1]
    n_blocks = seq // Q_BLOCK
    scale = ATT_QK_DIM ** -0.5
    q_blocks = q.reshape(bsz, n_blocks, Q_BLOCK, ATT_HEADS, 2, ATT_QK_DIM).swapaxes(0, 1)
    k_pos = jnp.arange(seq)

    def one_block(args):
        q_blk, blk = args
        s = jnp.einsum('bqhcd,bkhcd->bhcqk', q_blk, k).astype(jnp.float32) * scale
        q_pos = blk * Q_BLOCK + jnp.arange(Q_BLOCK)
        causal = k_pos[None, :] <= q_pos[:, None]
        s = jnp.where(causal, s, -jnp.inf)
        p = jax.nn.softmax(s, axis=-1)
        w = p[:, :, 0] - lam * p[:, :, 1]
        return jnp.einsum('bhqk,bkhe->bqhe', w.astype(v.dtype), v)

    out = lax.map(one_block, (q_blocks, jnp.arange(n_blocks)))
    return out.swapaxes(0, 1).reshape(bsz, seq, ATT_HEADS, ATT_V_DIM)


def s5_branch(u, lam_re, lam_im, log_dt, b_re, b_im, c_re, c_im, d_skip, w_glu, b_glu):
    f32 = jnp.float32
    bsz, seq, _ = u.shape
    uf = u.astype(f32).reshape(bsz, seq, SSM_GROUPS, SSM_GROUP)
    dt = jnp.exp(log_dt.astype(f32))[:, None]
    lr = lam_re.astype(f32)
    li = lam_im.astype(f32)
    mag = jnp.exp(lr * dt)
    ab_re = mag * jnp.cos(li * dt)
    ab_im = mag * jnp.sin(li * dt)
    nr = ab_re - 1.0
    ni = ab_im
    den = lr * lr + li * li
    coef_re = (nr * lr + ni * li) / den
    coef_im = (ni * lr - nr * li) / den
    br = b_re.astype(f32)
    bi = b_im.astype(f32)
    bb_re = coef_re[..., None] * br - coef_im[..., None] * bi
    bb_im = coef_re[..., None] * bi + coef_im[..., None] * br
    bu_re = jnp.einsum('bsgc,gpc->bsgp', uf, bb_re)
    bu_im = jnp.einsum('bsgc,gpc->bsgp', uf, bb_im)
    a_re = jnp.broadcast_to(ab_re[None, None], (1, seq, SSM_GROUPS, SSM_STATE))
    a_im = jnp.broadcast_to(ab_im[None, None], (1, seq, SSM_GROUPS, SSM_STATE))

    def combine(e1, e2):
        a1r, a1i, b1r, b1i = e1
        a2r, a2i, b2r, b2i = e2
        return (a2r * a1r - a2i * a1i,
                a2r * a1i + a2i * a1r,
                a2r * b1r - a2i * b1i + b2r,
                a2r * b1i + a2i * b1r + b2i)

    _, _, xs_re, xs_im = lax.associative_scan(combine, (a_re, a_im, bu_re, bu_im), axis=1)
    y = (jnp.einsum('bsgp,gcp->bsgc', xs_re, c_re.astype(f32))
         - jnp.einsum('bsgp,gcp->bsgc', xs_im, c_im.astype(f32)))
    y = y.reshape(bsz, seq, D_SSM) + d_skip.astype(f32) * u.astype(f32)
    y = jax.nn.gelu(y.astype(u.dtype))
    return y * jax.nn.sigmoid(y @ w_glu + b_glu)


def setup_inputs(seed: int = 0) -> dict:
    key = jax.random.key(seed)
    ks = jax.random.split(key, 24)
    f32 = jnp.float32
    L = DEPTH

    def nrm(k, shape, std):
        return jax.random.normal(k, shape, f32) * std

    n_idx = jnp.arange(SSM_STATE, dtype=f32)
    lam_re = -0.5 + nrm(ks[10], (L, SSM_GROUPS, SSM_STATE), 0.01)
    lam_im = math.pi * n_idx[None, None, :] + nrm(ks[11], (L, SSM_GROUPS, SSM_STATE), 0.01)
    log_dt = jax.random.uniform(ks[12], (L, SSM_GROUPS), f32,
                                minval=math.log(DT_MIN), maxval=math.log(DT_MAX))
    return {
        "x": jax.random.normal(ks[0], (BATCH, SEQ, D_MODEL), f32),
        "ln_gain": 1.0 + nrm(ks[1], (L, D_MODEL), 0.02),
        "w_in": nrm(ks[2], (L, D_MODEL, N_IN), D_MODEL ** -0.5),
        "q_norm_gain": 1.0 + nrm(ks[3], (L, ATT_QK_DIM), 0.02),
        "k_norm_gain": 1.0 + nrm(ks[4], (L, ATT_QK_DIM), 0.02),
        "lambda_q1": nrm(ks[5], (L, ATT_QK_DIM), 0.1),
        "lambda_k1": nrm(ks[6], (L, ATT_QK_DIM), 0.1),
        "lambda_q2": nrm(ks[7], (L, ATT_QK_DIM), 0.1),
        "lambda_k2": nrm(ks[8], (L, ATT_QK_DIM), 0.1),
        "subln_gain": 1.0 + nrm(ks[9], (L, ATT_V_DIM), 0.02),
        "ssm_lambda_re": lam_re,
        "ssm_lambda_im": lam_im,
        "ssm_log_dt": log_dt,
        "ssm_b_re": nrm(ks[13], (L, SSM_GROUPS, SSM_STATE, SSM_GROUP), (2.0 * SSM_GROUP) ** -0.5),
        "ssm_b_im": nrm(ks[14], (L, SSM_GROUPS, SSM_STATE, SSM_GROUP), (2.0 * SSM_GROUP) ** -0.5),
        "ssm_c_re": nrm(ks[15], (L, SSM_GROUPS, SSM_GROUP, SSM_STATE), (2.0 * SSM_STATE) ** -0.5),
        "ssm_c_im": nrm(ks[16], (L, SSM_GROUPS, SSM_GROUP, SSM_STATE), (2.0 * SSM_STATE) ** -0.5),
        "ssm_d": 1.0 + nrm(ks[17], (L, D_SSM), 0.1),
        "w_glu": nrm(ks[18], (L, D_SSM, D_SSM), D_SSM ** -0.5),
        "b_glu": nrm(ks[19], (L, D_SSM), 0.01),
        "w_proj_att": nrm(ks[20], (L, D_ATT, D_MODEL), D_ATT ** -0.5),
        "w_proj_ssm": nrm(ks[21], (L, D_SSM, D_MODEL), D_SSM ** -0.5),
        "w_out": nrm(ks[22], (L, D_MODEL, D_MODEL), D_MODEL ** -0.5),
    }


def reference(x, ln_gain, w_in, q_norm_gain, k_norm_gain, lambda_q1, lambda_k1, lambda_q2,
              lambda_k2, subln_gain, ssm_lambda_re, ssm_lambda_im, ssm_log_dt, ssm_b_re,
              ssm_b_im, ssm_c_re, ssm_c_im, ssm_d, w_glu, b_glu, w_proj_att, w_proj_ssm, w_out):
    bsz, seq, _ = x.shape
    splits = [int(s) for s in np.cumsum(IN_SIZES)[:-1]]
    for l in range(DEPTH):
        lam_init = lambda_init_fn(l)
        h = rms_norm(x, ln_gain[l])
        proj = h @ w_in[l]
        q, k, v, z_att, u, z_ssm, g_att, g_ssm = jnp.split(proj, splits, axis=-1)

        q = rms_norm(q.reshape(bsz, seq, ATT_HEADS, 2, ATT_QK_DIM), q_norm_gain[l])
        k = rms_norm(k.reshape(bsz, seq, ATT_HEADS, 2, ATT_QK_DIM), k_norm_gain[l])
        v = v.reshape(bsz, seq, ATT_HEADS, ATT_V_DIM)
        lam = (jnp.exp(jnp.sum(lambda_q1[l].astype(jnp.float32) * lambda_k1[l].astype(jnp.float32)))
               - jnp.exp(jnp.sum(lambda_q2[l].astype(jnp.float32) * lambda_k2[l].astype(jnp.float32)))
               + lam_init)
        attn = diff_attention(q, k, v, lam)
        attn = rms_norm(attn, subln_gain[l]) * (1.0 - lam_init)
        y_att = attn.reshape(bsz, seq, D_ATT) * jax.nn.silu(z_att)

        y_ssm = s5_branch(u, ssm_lambda_re[l], ssm_lambda_im[l], ssm_log_dt[l], ssm_b_re[l],
                          ssm_b_im[l], ssm_c_re[l], ssm_c_im[l], ssm_d[l], w_glu[l], b_glu[l])
        y_ssm = y_ssm.astype(x.dtype) * jax.nn.silu(z_ssm)

        merged = (jax.nn.sigmoid(g_att) * (y_att @ w_proj_att[l])
                  + jax.nn.sigmoid(g_ssm) * (y_ssm @ w_proj_ssm[l]))
        x = x + merged @ w_out[l]
    return x
```

```python
import functools
import math

import jax
import jax.numpy as jnp
from jax import lax
from jax.experimental import pallas as pl
from jax.experimental.pallas import tpu as pltpu

F32 = jnp.float32
BF16 = jnp.bfloat16
HIGHEST = lax.Precision.HIGHEST

D_MODEL = 2048
HEADS = 8
DQK = 64
DV = 2 * DQK
D_ATT = HEADS * DV
D_SSM = 1024
GROUP = 16
GROUPS = D_SSM // GROUP
PAIRS = GROUPS // 2
STATE = 64
N_IN = 6 * 1024 + 2 * D_MODEL
RMS_EPS = 1e-6
LAMBDA_INIT = 0.8 - 0.6 * math.exp(-0.3 * 0)
CHUNK = 16
CW = CHUNK * GROUP
SUBLANES = 8
NEG = -1e30

TILE_Q, TILE_K, TILE_V, TILE_ZATT, TILE_U, TILE_ZSSM = 0, 1, 2, 3, 4, 5
TILE_GATT0, TILE_GSSM0 = 6, 8
N_TILES = N_IN // 1024

VMEM_LIMIT = 56 * 1024 * 1024


def _dot(a, b):
    return jnp.dot(a, b, preferred_element_type=F32)


def _in_proj_kernel(x_ref, ln_ref, w_ref, qg_ref, kg_ref, gsum_ref, proj_ref, vt_ref, h_ref):
    j = pl.program_id(1)

    @pl.when(j == 0)
    def _():
        x = x_ref[...]
        ms = jnp.mean(x * x, axis=-1, keepdims=True)
        h_ref[...] = (x * lax.rsqrt(ms + RMS_EPS) * ln_ref[...]).astype(BF16)

    acc = _dot(h_ref[...], w_ref[...])

    def group_rms_norm(gain):
        sq = acc * acc
        hi = sq.astype(BF16)
        lo = (sq - hi.astype(F32)).astype(BF16)
        g = gsum_ref[...]
        cols = []
        for c in range(4):
            sl = slice(c * 256, (c + 1) * 256)
            cols.append(_dot(hi[:, sl], g) + _dot(lo[:, sl], g))
        ms = jnp.concatenate(cols, axis=1) * (1.0 / DQK)
        return acc * lax.rsqrt(ms + RMS_EPS) * gain

    @pl.when(j == TILE_Q)
    def _():
        proj_ref[...] = group_rms_norm(qg_ref[...]).astype(BF16)

    @pl.when(j == TILE_K)
    def _():
        proj_ref[...] = group_rms_norm(kg_ref[...]).astype(BF16)

    @pl.when(j == TILE_V)
    def _():
        proj_ref[...] = acc.astype(BF16)
        vt_ref[...] = acc.T.astype(BF16)

    @pl.when(j == TILE_U)
    def _():
        proj_ref[...] = acc.astype(BF16)

    @pl.when((j == TILE_ZATT) | (j == TILE_ZSSM))
    def _():
        proj_ref[...] = jax.nn.silu(acc).astype(BF16)

    @pl.when(j >= TILE_GATT0)
    def _():
        proj_ref[...] = jax.nn.sigmoid(acc).astype(BF16)


def _in_proj(x2, ln_gain, w_in, q_gain, k_gain, bsz, seq, tm):
    tokens = bsz * seq
    tiles_per_seq = seq // tm
    gi = lax.broadcasted_iota(jnp.int32, (256, 256), 0) // DQK
    gj = lax.broadcasted_iota(jnp.int32, (256, 256), 1) // DQK
    gsum = (gi == gj).astype(BF16)
    return pl.pallas_call(
        _in_proj_kernel,
        grid=(tokens // tm, N_TILES),
        in_specs=[
            pl.BlockSpec((tm, D_MODEL), lambda i, j: (i, 0)),
            pl.BlockSpec((1, D_MODEL), lambda i, j: (0, 0)),
            pl.BlockSpec((D_MODEL, 1024), lambda i, j: (0, j)),
            pl.BlockSpec((1, 1024), lambda i, j: (0, 0)),
            pl.BlockSpec((1, 1024), lambda i, j: (0, 0)),
            pl.BlockSpec((256, 256), lambda i, j: (0, 0)),
        ],
        out_specs=[
            pl.BlockSpec((tm, 1024), lambda i, j: (i, j)),
            pl.BlockSpec((None, D_ATT, tm), lambda i, j: (i // tiles_per_seq, 0, i % tiles_per_seq)),
        ],
        out_shape=[
            jax.ShapeDtypeStruct((tokens, N_IN), BF16),
            jax.ShapeDtypeStruct((bsz, D_ATT, seq), BF16),
        ],
        scratch_shapes=[pltpu.VMEM((tm, D_MODEL), BF16)],
        compiler_params=pltpu.CompilerParams(
            dimension_semantics=("arbitrary", "arbitrary"), vmem_limit_bytes=VMEM_LIMIT),
        name="in_proj",
    )(x2, ln_gain, w_in, q_gain, k_gain, gsum)


def _attn_kernel(lp_ref, sg_ref, q_ref, k_ref, vt_ref, z_ref, o_ref,
                 qbd_ref, m_ref, l_ref, acc_ref, *, seq, tq):
    lp = lp_ref[...]
    lam = (jnp.exp(jnp.sum(lp[0:1] * lp[1:2], axis=-1, keepdims=True))
           - jnp.exp(jnp.sum(lp[2:3] * lp[3:4], axis=-1, keepdims=True)) + LAMBDA_INIT)

    def q_tile(i, carry):
        q0 = pl.multiple_of(i * tq, tq)
        qt = q_ref[pl.ds(q0, tq), :].astype(F32).T
        row = lax.broadcasted_iota(jnp.int32, (DV, tq), 0)
        zero = jnp.zeros_like(qt)
        qbd_ref[:, :tq] = jnp.where(row < DQK, qt, zero).astype(BF16)
        qbd_ref[:, tq:] = jnp.where(row >= DQK, qt, zero).astype(BF16)
        m_ref[...] = jnp.full(m_ref.shape, NEG, F32)
        l_ref[...] = jnp.zeros(l_ref.shape, F32)
        acc_ref[...] = jnp.zeros(acc_ref.shape, F32)

        def block(j, masked):
            k0 = pl.multiple_of(j * tq, tq)
            s = _dot(k_ref[pl.ds(k0, tq), :], qbd_ref[...])
            if masked:
                kpos = lax.broadcasted_iota(jnp.int32, (tq, 2 * tq), 0)
                qpos = lax.broadcasted_iota(jnp.int32, (tq, 2 * tq), 1)
                qpos = jnp.where(qpos >= tq, qpos - tq, qpos)
                s = jnp.where(kpos <= qpos, s, NEG)
            m_old = m_ref[...]
            m_new = jnp.maximum(m_old, jnp.max(s, axis=0, keepdims=True))
            alpha = jnp.exp(m_old - m_new)
            p = jnp.exp(s - m_new)
            l_ref[...] = alpha * l_ref[...] + jnp.sum(p, axis=0, keepdims=True)
            m_ref[...] = m_new
            pv = _dot(vt_ref[:, pl.ds(k0, tq)], p.astype(BF16))
            acc_ref[...] = alpha * acc_ref[...] + pv

        def unmasked(j, c):
            block(j, False)
            return c

        lax.fori_loop(0, i, unmasked, 0)
        block(i, True)

        o = acc_ref[...] * (1.0 / l_ref[...])
        a = o[:, :tq] - lam * o[:, tq:]
        ms = jnp.mean(a * a, axis=0, keepdims=True)
        n = (a * lax.rsqrt(ms + RMS_EPS)).T
        out = n * sg_ref[...] * z_ref[pl.ds(q0, tq), :].astype(F32)
        o_ref[pl.ds(q0, tq), :] = out.astype(BF16)
        return carry

    lax.fori_loop(0, seq // tq, q_tile, 0)


def _attention(lam_params, subln, proj3, vt, bsz, seq, tq):
    kern = functools.partial(_attn_kernel, seq=seq, tq=tq)
    qcol, kcol, zcol = TILE_Q * HEADS, TILE_K * HEADS, TILE_ZATT * HEADS
    return pl.pallas_call(
        kern,
        grid=(bsz, HEADS),
        in_specs=[
            pl.BlockSpec((4, DQK), lambda b, h: (0, 0)),
            pl.BlockSpec((1, DV), lambda b, h: (0, 0)),
            pl.BlockSpec((None, seq, DV), lambda b, h: (b, 0, qcol + h)),
            pl.BlockSpec((None, seq, DV), lambda b, h: (b, 0, kcol + h)),
            pl.BlockSpec((None, DV, seq), lambda b, h: (b, h, 0)),
            pl.BlockSpec((None, seq, DV), lambda b, h: (b, 0, zcol + h)),
        ],
        out_specs=pl.BlockSpec((None, seq, DV), lambda b, h: (b, 0, h)),
        out_shape=jax.ShapeDtypeStruct((bsz, seq, D_ATT), BF16),
        scratch_shapes=[
            pltpu.VMEM((DV, 2 * tq), BF16),
            pltpu.VMEM((1, 2 * tq), F32),
            pltpu.VMEM((1, 2 * tq), F32),
            pltpu.VMEM((DV, 2 * tq), F32),
        ],
        compiler_params=pltpu.CompilerParams(
            dimension_semantics=("arbitrary", "arbitrary"), vmem_limit_bytes=VMEM_LIMIT),
        name="diff_attention",
    )(lam_params, subln, proj3, proj3, vt, proj3)


def _ssm_param_kernel(ldt_ref, lre_ref, lim_ref, lrec_ref, limc_ref, bre_ref, bim_ref,
                      ctre_ref, ctim_ref, t_ref, w_ref, v_ref):
    h = pl.program_id(0) % 2
    dt = jnp.exp(ldt_ref[...])
    lr, li = lre_ref[...], lim_ref[...]

    def cis_pow(n, lr_, li_):
        mag = jnp.exp(n * (lr_ * dt))
        ang = n * (li_ * dt)
        return mag * jnp.cos(ang), mag * jnp.sin(ang)

    a_re, a_im = cis_pow(1.0, lr, li)
    nr, ni = a_re - 1.0, a_im
    den = lr * lr + li * li
    coef_re = (nr * lr + ni * li) / den
    coef_im = (ni * lr - nr * li) / den
    br, bi = bre_ref[...], bim_ref[...]
    bb_re = coef_re * br - coef_im * bi
    bb_im = coef_re * bi + coef_im * br

    n_rows = lax.broadcasted_iota(jnp.int32, (CHUNK, CW), 0).astype(F32)
    p_re, p_im = cis_pow(n_rows, lr, li)
    slot = lax.broadcasted_iota(jnp.int32, (GROUP, CW), 1) // STATE
    for s in range(CHUNK):
        r_re = p_re[CHUNK - 1 - s:CHUNK - s]
        r_im = p_im[CHUNK - 1 - s:CHUNK - s]
        w_re = bb_re * r_re - bb_im * r_im
        w_im = bb_re * r_im + bb_im * r_re
        val = jnp.where(slot == h, w_re, jnp.where(slot == h + 2, w_im, 0.0))
        w_ref[s * GROUP:(s + 1) * GROUP, :] = val.astype(BF16)

    lrc, lic = lrec_ref[...], limc_ref[...]
    n_lanes = lax.broadcasted_iota(jnp.int32, (STATE, 128), 1).astype(F32)
    q_re, q_im = cis_pow(n_lanes, lrc, lic)
    sel_n = lax.broadcasted_iota(jnp.int32, (128, CW), 0)
    sel_t = lax.broadcasted_iota(jnp.int32, (128, CW), 1) // GROUP
    sel_c = lax.broadcasted_iota(jnp.int32, (128, CW), 1) % GROUP
    e_t0 = (sel_n == sel_t).astype(F32)
    e_t1 = (sel_n == sel_t + 1).astype(F32)
    e_co = (sel_n == sel_c).astype(F32)

    def expand(a, e):
        return jnp.dot(a, e, preferred_element_type=F32, precision=HIGHEST)

    c_re = expand(ctre_ref[...], e_co)
    c_im = expand(ctim_ref[...], e_co)
    p0_re, p0_im = expand(q_re, e_t0), expand(q_im, e_t0)
    p1_re, p1_im = expand(q_re, e_t1), expand(q_im, e_t1)

    r_re = p0_re * c_re - p0_im * c_im
    r_im = p0_re * c_im + p0_im * c_re
    bb64_re, bb64_im = bb_re[:, :STATE], bb_im[:, :STATE]
    kall = (jnp.dot(bb64_re, r_re, preferred_element_type=F32, precision=HIGHEST)
            - jnp.dot(bb64_im, r_im, preferred_element_type=F32, precision=HIGHEST))
    lane = lax.broadcasted_iota(jnp.int32, (GROUP, CW), 1)
    for s in range(CHUNK):
        shifted = kall if s == 0 else pltpu.roll(kall, s * GROUP, axis=1)
        t_ref[s * GROUP:(s + 1) * GROUP, :] = jnp.where(lane >= s * GROUP, shifted, 0.0).astype(BF16)

    v_re = p1_re * c_re - p1_im * c_im
    v_im = -(p1_re * c_im + p1_im * c_re)
    v_ref[...] = jnp.zeros(v_ref.shape, BF16)
    v_ref[pl.ds(pl.multiple_of(h * STATE, STATE), STATE), :] = v_re.astype(BF16)
    v_ref[pl.ds(pl.multiple_of((h + 2) * STATE, STATE), STATE), :] = v_im.astype(BF16)


def _ssm_params(log_dt, lam_re, lam_im, b_re, b_im, c_re, c_im):
    tile4 = lambda a: jnp.tile(a, (1, 1, 4))
    ldt = log_dt.reshape(GROUPS, 1, 1)
    lre4 = tile4(lam_re.reshape(GROUPS, 1, STATE))
    lim4 = tile4(lam_im.reshape(GROUPS, 1, STATE))
    lrec = lam_re.reshape(GROUPS, STATE, 1)
    limc = lam_im.reshape(GROUPS, STATE, 1)
    bt_re = tile4(jnp.swapaxes(b_re, 1, 2))
    bt_im = tile4(jnp.swapaxes(b_im, 1, 2))
    pad = lambda a: jnp.pad(jnp.swapaxes(a, 1, 2), ((0, 0), (0, 0), (0, 128 - GROUP)))
    ct_re, ct_im = pad(c_re), pad(c_im)

    def spec(shape):
        return pl.BlockSpec((None,) + shape, lambda g: (g, 0, 0))

    return pl.pallas_call(
        _ssm_param_kernel,
        grid=(GROUPS,),
        in_specs=[spec((1, 1)), spec((1, CW)), spec((1, CW)), spec((STATE, 1)), spec((STATE, 1)),
                  spec((GROUP, CW)), spec((GROUP, CW)), spec((STATE, 128)), spec((STATE, 128))],
        out_specs=[
            pl.BlockSpec((None, CW, CW), lambda g: (g, 0, 0)),
            pl.BlockSpec((None, CW, CW), lambda g: (g // 2, g % 2, 0)),
            pl.BlockSpec((None, CW, CW), lambda g: (g // 2, 0, g % 2)),
        ],
        out_shape=[
            jax.ShapeDtypeStruct((GROUPS, CW, CW), BF16),
            jax.ShapeDtypeStruct((PAIRS, 2 * CW, CW), BF16),
            jax.ShapeDtypeStruct((PAIRS, CW, 2 * CW), BF16),
        ],
        compiler_params=pltpu.CompilerParams(dimension_semantics=("arbitrary",)),
        name="ssm_params",
    )(ldt, lre4, lim4, lrec, limc, bt_re, bt_im, ct_re, ct_im)


def _ssm_kernel(u_ref, t_ref, w_ref, v_ref, d_ref, ldt_ref, lre_ref, lim_ref, y_ref,
                s_ref, xr_ref, xi_ref, *, bsz):
    rows = u_ref.shape[1]

    @pl.when(pl.program_id(0) == 0)
    def _():
        xr_ref[...] = jnp.zeros(xr_ref.shape, F32)
        xi_ref[...] = jnp.zeros(xi_ref.shape, F32)

    def contrib(gp, c):
        g0 = 2 * gp
        u_pair = jnp.concatenate([u_ref[g0], u_ref[g0 + 1]], axis=1)
        s_ref[gp] = _dot(u_pair, w_ref[gp])
        return c

    lax.fori_loop(0, PAIRS, contrib, 0)

    dt = jnp.exp(ldt_ref[...])
    mag = jnp.exp(float(CHUNK) * (lre_ref[...] * dt))
    ang = float(CHUNK) * (lim_ref[...] * dt)
    a_re, a_im = mag * jnp.cos(ang), mag * jnp.sin(ang)
    sub = lax.broadcasted_iota(jnp.int32, (PAIRS, SUBLANES, 128), 1)

    def scan_tile(n, c):
        r0 = pl.multiple_of(n * SUBLANES, SUBLANES)
        tile = s_ref[:, pl.ds(r0, SUBLANES), :]
        s_re, s_im = tile[:, :, :128], tile[:, :, 128:]
        x_re, x_im = xr_ref[...], xi_ref[...]
        in_re, in_im = x_re, x_im
        for step in range(SUBLANES // bsz):
            if step:
                x_re = pltpu.roll(x_re, bsz, axis=1)
                x_im = pltpu.roll(x_im, bsz, axis=1)
                live = (sub >= step * bsz) & (sub < (step + 1) * bsz)
                in_re = jnp.where(live, x_re, in_re)
                in_im = jnp.where(live, x_im, in_im)
            x_re, x_im = (a_re * x_re - a_im * x_im + s_re,
                          a_re * x_im + a_im * x_re + s_im)
        s_ref[:, pl.ds(r0, SUBLANES), :] = jnp.concatenate([in_re, in_im], axis=2)
        if SUBLANES // bsz > 1:
            x_re = pltpu.roll(x_re, bsz, axis=1)
            x_im = pltpu.roll(x_im, bsz, axis=1)
        xr_ref[...] = x_re
        xi_ref[...] = x_im
        return c

    lax.fori_loop(0, rows // SUBLANES, scan_tile, 0)

    def emit(gp, c):
        g0 = 2 * gp
        y_state = _dot(s_ref[gp].astype(BF16), v_ref[gp])
        for hh in range(2):
            u = u_ref[g0 + hh]
            y = _dot(u, t_ref[g0 + hh]) + y_state[:, hh * CW:(hh + 1) * CW] + d_ref[g0 + hh] * u.astype(F32)
            y_ref[g0 + hh] = jax.nn.gelu(y).astype(BF16)
        return c

    lax.fori_loop(0, PAIRS, emit, 0)


def _ssm(u3, t_mat, w_mat, v_mat, d_chunk, ldt_p, lre_p, lim_p, bsz, rows_per_step):
    total_rows = u3.shape[1]
    kern = functools.partial(_ssm_kernel, bsz=bsz)
    const3 = lambda t: (0, 0, 0)
    once = pl.Buffered(1)
    return pl.pallas_call(
        kern,
        grid=(total_rows // rows_per_step,),
        in_specs=[
            pl.BlockSpec((GROUPS, rows_per_step, CW), lambda t: (0, t, 0)),
            pl.BlockSpec((GROUPS, CW, CW), const3, pipeline_mode=once),
            pl.BlockSpec((PAIRS, 2 * CW, CW), const3, pipeline_mode=once),
            pl.BlockSpec((PAIRS, CW, 2 * CW), const3, pipeline_mode=once),
            pl.BlockSpec((GROUPS, 1, CW), const3, pipeline_mode=once),
            pl.BlockSpec((PAIRS, 1, 128), const3, pipeline_mode=once),
            pl.BlockSpec((PAIRS, 1, 128), const3, pipeline_mode=once),
            pl.BlockSpec((PAIRS, 1, 128), const3, pipeline_mode=once),
        ],
        out_specs=pl.BlockSpec((GROUPS, rows_per_step, CW), lambda t: (0, t, 0)),
        out_shape=jax.ShapeDtypeStruct((GROUPS, total_rows, CW), BF16),
        scratch_shapes=[
            pltpu.VMEM((PAIRS, rows_per_step, 2 * 128), F32),
            pltpu.VMEM((PAIRS, SUBLANES, 128), F32),
            pltpu.VMEM((PAIRS, SUBLANES, 128), F32),
        ],
        compiler_params=pltpu.CompilerParams(
            dimension_semantics=("arbitrary",), vmem_limit_bytes=VMEM_LIMIT),
        name="ssm_scan",
    )(u3, t_mat, w_mat, v_mat, d_chunk, ldt_p, lre_p, lim_p)


def _merge_kernel(x_ref, ya_ref, yg_ref, zs_ref, ga_ref, gs_ref, wg_ref, bg_ref,
                  wpa_ref, wps_ref, wo_ref, o_ref):
    yg = yg_ref[...]
    glu = _dot(yg, wg_ref[...]) + bg_ref[...]
    y_ssm = (yg.astype(F32) * jax.nn.sigmoid(glu) * zs_ref[...].astype(F32)).astype(BF16)
    merged = (ga_ref[...].astype(F32) * _dot(ya_ref[...], wpa_ref[...])
              + gs_ref[...].astype(F32) * _dot(y_ssm, wps_ref[...]))
    o_ref[...] = x_ref[...] + _dot(merged.astype(BF16), wo_ref[...])


def _merge(x2, y_att, yg, proj, w_glu, b_glu, w_pa, w_ps, w_out, tm):
    tokens = x2.shape[0]
    const = lambda i: (0, 0)
    once = pl.Buffered(1)
    return pl.pallas_call(
        _merge_kernel,
        grid=(tokens // tm,),
        in_specs=[
            pl.BlockSpec((tm, D_MODEL), lambda i: (i, 0)),
            pl.BlockSpec((tm, D_ATT), lambda i: (i, 0)),
            pl.BlockSpec((tm, D_SSM), lambda i: (i, 0)),
            pl.BlockSpec((tm, D_SSM), lambda i: (i, TILE_ZSSM)),
            pl.BlockSpec((tm, D_MODEL), lambda i: (i, TILE_GATT0 // 2)),
            pl.BlockSpec((tm, D_MODEL), lambda i: (i, TILE_GSSM0 // 2)),
            pl.BlockSpec((D_SSM, D_SSM), const, pipeline_mode=once),
            pl.BlockSpec((1, D_SSM), const, pipeline_mode=once),
            pl.BlockSpec((D_ATT, D_MODEL), const, pipeline_mode=once),
            pl.BlockSpec((D_SSM, D_MODEL), const, pipeline_mode=once),
            pl.BlockSpec((D_MODEL, D_MODEL), const, pipeline_mode=once),
        ],
        out_specs=pl.BlockSpec((tm, D_MODEL), lambda i: (i, 0)),
        out_shape=jax.ShapeDtypeStruct((tokens, D_MODEL), F32),
        compiler_params=pltpu.CompilerParams(
            dimension_semantics=("arbitrary",), vmem_limit_bytes=VMEM_LIMIT),
        name="merge_out",
    )(x2, y_att, yg, proj, proj, proj, w_glu, b_glu, w_pa, w_ps, w_out)


def kernel(x, ln_gain, w_in, q_norm_gain, k_norm_gain, lambda_q1, lambda_k1, lambda_q2, lambda_k2,
           subln_gain, ssm_lambda_re, ssm_lambda_im, ssm_log_dt, ssm_b_re, ssm_b_im, ssm_c_re,
           ssm_c_im, ssm_d, w_glu, b_glu, w_proj_att, w_proj_ssm, w_out):
    bsz, seq, _ = x.shape
    assert ln_gain.shape[0] == 1 and x.shape[2] == D_MODEL and w_in.shape[2] == N_IN
    assert SUBLANES % bsz == 0 and seq % (CHUNK * SUBLANES) == 0
    tokens = bsz * seq
    n_chunks = seq // CHUNK
    tm_in = min(1024, seq)
    tq = min(256, seq)
    tm_out = min(256, seq)
    rows_per_step = min(32, n_chunks) * bsz

    x2 = x.reshape(tokens, D_MODEL)
    scale = DQK ** -0.5
    q_gain = (jnp.tile(q_norm_gain[0], 2 * HEADS) * scale).reshape(1, 1024)
    k_gain = jnp.tile(k_norm_gain[0], 2 * HEADS).reshape(1, 1024)
    proj, vt = _in_proj(x2, ln_gain[0].reshape(1, D_MODEL), w_in[0].astype(BF16), q_gain, k_gain,
                        bsz, seq, tm_in)

    lam_params = jnp.stack([lambda_q1[0], lambda_k1[0], lambda_q2[0], lambda_k2[0]])
    subln = (subln_gain[0] * (1.0 - LAMBDA_INIT)).reshape(1, DV)
    y_att = _attention(lam_params, subln, proj.reshape(bsz, seq, N_IN), vt, bsz, seq, tq)

    t_mat, w_mat, v_mat = _ssm_params(ssm_log_dt[0], ssm_lambda_re[0], ssm_lambda_im[0],
                                      ssm_b_re[0], ssm_b_im[0], ssm_c_re[0], ssm_c_im[0])
    u = proj[:, TILE_U * 1024:(TILE_U + 1) * 1024]
    u3 = (u.reshape(bsz, n_chunks, CHUNK, GROUPS, GROUP).transpose(3, 1, 0, 2, 4)
          .reshape(GROUPS, n_chunks * bsz, CW))
    d_chunk = jnp.tile(ssm_d[0].reshape(GROUPS, 1, GROUP), (1, 1, CHUNK))
    ldt_p = jnp.repeat(ssm_log_dt[0], STATE).reshape(PAIRS, 1, 128)
    lre_p = ssm_lambda_re[0].reshape(PAIRS, 1, 128)
    lim_p = ssm_lambda_im[0].reshape(PAIRS, 1, 128)
    yg3 = _ssm(u3, t_mat, w_mat, v_mat, d_chunk, ldt_p, lre_p, lim_p, bsz, rows_per_step)
    yg = (yg3.reshape(GROUPS, n_chunks, bsz, CHUNK, GROUP).transpose(2, 1, 3, 0, 4)
          .reshape(tokens, D_SSM))

    out = _merge(x2, y_att.reshape(tokens, D_ATT), yg, proj, w_glu[0].astype(BF16),
                 b_glu[0].reshape(1, D_SSM), w_proj_att[0].astype(BF16), w_proj_ssm[0].astype(BF16),
                 w_out[0].astype(BF16), tm_out)
    return out.reshape(bsz, seq, D_MODEL)
```

```python
import functools
import math

import jax
import jax.numpy as jnp
from jax import lax
from jax.experimental import pallas as pl
from jax.experimental.pallas import tpu as pltpu

F32 = jnp.float32
BF16 = jnp.bfloat16
HIGHEST = lax.Precision.HIGHEST

D_MODEL = 2048
HEADS = 8
DQK = 64
DV = 2 * DQK
D_ATT = HEADS * DV
D_SSM = 1024
GROUP = 16
GROUPS = D_SSM // GROUP
PAIRS = GROUPS // 2
STATE = 64
N_IN = 6 * 1024 + 2 * D_MODEL
RMS_EPS = 1e-6
LAMBDA_INIT = 0.8 - 0.6 * math.exp(-0.3 * 0)
CHUNK = 16
CW = CHUNK * GROUP
SUBLANES = 8
NEG = -1e30
SCORE_BOUND_FAST = 30.0
KV_GROUP = 4

TILE_Q, TILE_K, TILE_V, TILE_ZATT, TILE_U, TILE_ZSSM = 0, 1, 2, 3, 4, 5
TILE_GATT0, TILE_GSSM0 = 6, 8
N_TILES = N_IN // 1024

VMEM_LIMIT = 56 * 1024 * 1024


def _dot(a, b):
    return jnp.dot(a, b, preferred_element_type=F32)


def _in_proj_kernel(x_ref, ln_ref, w_ref, qg_ref, kg_ref, gsum_ref, proj_ref, vt_ref, h_ref):
    j = pl.program_id(1)

    @pl.when(j == 0)
    def _():
        x = x_ref[...]
        ms = jnp.mean(x * x, axis=-1, keepdims=True)
        h_ref[...] = (x * lax.rsqrt(ms + RMS_EPS) * ln_ref[...]).astype(BF16)

    acc = _dot(h_ref[...], w_ref[...])

    def group_rms_norm(gain):
        sq = acc * acc
        hi = sq.astype(BF16)
        lo = (sq - hi.astype(F32)).astype(BF16)
        g = gsum_ref[...]
        cols = []
        for c in range(4):
            sl = slice(c * 256, (c + 1) * 256)
            cols.append(_dot(hi[:, sl], g) + _dot(lo[:, sl], g))
        ms = jnp.concatenate(cols, axis=1) * (1.0 / DQK)
        return acc * lax.rsqrt(ms + RMS_EPS) * gain

    @pl.when(j == TILE_Q)
    def _():
        proj_ref[...] = group_rms_norm(qg_ref[...]).astype(BF16)

    @pl.when(j == TILE_K)
    def _():
        proj_ref[...] = group_rms_norm(kg_ref[...]).astype(BF16)

    @pl.when(j == TILE_V)
    def _():
        proj_ref[...] = acc.astype(BF16)
        vt_ref[...] = acc.T.astype(BF16)

    @pl.when(j == TILE_U)
    def _():
        proj_ref[...] = acc.astype(BF16)

    @pl.when((j == TILE_ZATT) | (j == TILE_ZSSM))
    def _():
        proj_ref[...] = jax.nn.silu(acc).astype(BF16)

    @pl.when(j >= TILE_GATT0)
    def _():
        proj_ref[...] = jax.nn.sigmoid(acc).astype(BF16)


def _in_proj(x2, ln_gain, w_in, q_gain, k_gain, bsz, seq, tm):
    tokens = bsz * seq
    tiles_per_seq = seq // tm
    gi = lax.broadcasted_iota(jnp.int32, (256, 256), 0) // DQK
    gj = lax.broadcasted_iota(jnp.int32, (256, 256), 1) // DQK
    gsum = (gi == gj).astype(BF16)
    return pl.pallas_call(
        _in_proj_kernel,
        grid=(tokens // tm, N_TILES),
        in_specs=[
            pl.BlockSpec((tm, D_MODEL), lambda i, j: (i, 0)),
            pl.BlockSpec((1, D_MODEL), lambda i, j: (0, 0)),
            pl.BlockSpec((D_MODEL, 1024), lambda i, j: (0, j)),
            pl.BlockSpec((1, 1024), lambda i, j: (0, 0)),
            pl.BlockSpec((1, 1024), lambda i, j: (0, 0)),
            pl.BlockSpec((256, 256), lambda i, j: (0, 0)),
        ],
        out_specs=[
            pl.BlockSpec((tm, 1024), lambda i, j: (i, j)),
            pl.BlockSpec((None, D_ATT, tm), lambda i, j: (i // tiles_per_seq, 0, i % tiles_per_seq)),
        ],
        out_shape=[
            jax.ShapeDtypeStruct((tokens, N_IN), BF16),
            jax.ShapeDtypeStruct((bsz, D_ATT, seq), BF16),
        ],
        scratch_shapes=[pltpu.VMEM((tm, D_MODEL), BF16)],
        compiler_params=pltpu.CompilerParams(
            dimension_semantics=("arbitrary", "arbitrary"), vmem_limit_bytes=VMEM_LIMIT),
        name="in_proj",
    )(x2, ln_gain, w_in, q_gain, k_gain, gsum)


def _attn_kernel(lp_ref, sg_ref, q_ref, k_ref, vt_ref, z_ref, o_ref,
                 qbd_ref, m_ref, l_ref, acc_ref, *, seq, tq, fast):
    lp = lp_ref[...]
    lam = (jnp.exp(jnp.sum(lp[0:1] * lp[1:2], axis=-1, keepdims=True))
           - jnp.exp(jnp.sum(lp[2:3] * lp[3:4], axis=-1, keepdims=True)) + LAMBDA_INIT)

    def q_tile(i, carry):
        q0 = pl.multiple_of(i * tq, tq)
        qt = q_ref[pl.ds(q0, tq), :].astype(F32).T
        row = lax.broadcasted_iota(jnp.int32, (DV, tq), 0)
        zero = jnp.zeros_like(qt)
        qbd_ref[:, :tq] = jnp.where(row < DQK, qt, zero).astype(BF16)
        qbd_ref[:, tq:] = jnp.where(row >= DQK, qt, zero).astype(BF16)
        m_ref[...] = jnp.full(m_ref.shape, NEG, F32)
        l_ref[...] = jnp.zeros(l_ref.shape, F32)
        acc_ref[...] = jnp.zeros(acc_ref.shape, F32)

        def block(j, masked, state):
            m, l, acc = state
            k0 = pl.multiple_of(j * tq, tq)
            s = _dot(k_ref[pl.ds(k0, tq), :], qbd_ref[...])
            if masked:
                kpos = lax.broadcasted_iota(jnp.int32, (tq, 2 * tq), 0)
                qpos = lax.broadcasted_iota(jnp.int32, (tq, 2 * tq), 1)
                qpos = jnp.where(qpos >= tq, qpos - tq, qpos)
                s = jnp.where(kpos <= qpos, s, NEG)
            vt = vt_ref[:, pl.ds(k0, tq)]
            if fast:
                p = jnp.exp(s)
                return m, l + jnp.sum(p, axis=0, keepdims=True), acc + _dot(vt, p.astype(BF16))
            m_new = jnp.maximum(m, jnp.max(s, axis=0, keepdims=True))
            alpha = jnp.exp(m - m_new)
            p = jnp.exp(s - m_new)
            return (m_new, alpha * l + jnp.sum(p, axis=0, keepdims=True),
                    alpha * acc + _dot(vt, p.astype(BF16)))

        def load_state():
            return m_ref[...], l_ref[...], acc_ref[...]

        def group(jj, c):
            state = load_state()
            for u in range(KV_GROUP):
                state = block(jj * KV_GROUP + u, False, state)
            m_ref[...], l_ref[...], acc_ref[...] = state
            return c

        n_groups = lax.shift_right_logical(i, KV_GROUP.bit_length() - 1)
        lax.fori_loop(0, n_groups, group, 0)

        for rem in range(KV_GROUP):
            @pl.when((i & (KV_GROUP - 1)) == rem)
            def _():
                state = load_state()
                for u in range(rem):
                    state = block(n_groups * KV_GROUP + u, False, state)
                _, l, acc = block(i, True, state)
                o = acc * (1.0 / l)
                a = o[:, :tq] - lam * o[:, tq:]
                ms = jnp.mean(a * a, axis=0, keepdims=True)
                n = (a * lax.rsqrt(ms + RMS_EPS)).T
                out = n * sg_ref[...] * z_ref[pl.ds(q0, tq), :].astype(F32)
                o_ref[pl.ds(q0, tq), :] = out.astype(BF16)
        return carry

    lax.fori_loop(0, seq // tq, q_tile, 0)


def _attention(lam_params, subln, proj3, vt, *, bsz, seq, tq, fast):
    kern = functools.partial(_attn_kernel, seq=seq, tq=tq, fast=fast)
    qcol, kcol, zcol = TILE_Q * HEADS, TILE_K * HEADS, TILE_ZATT * HEADS
    return pl.pallas_call(
        kern,
        grid=(bsz, HEADS),
        in_specs=[
            pl.BlockSpec((4, DQK), lambda b, h: (0, 0)),
            pl.BlockSpec((1, DV), lambda b, h: (0, 0)),
            pl.BlockSpec((None, seq, DV), lambda b, h: (b, 0, qcol + h)),
            pl.BlockSpec((None, seq, DV), lambda b, h: (b, 0, kcol + h)),
            pl.BlockSpec((None, DV, seq), lambda b, h: (b, h, 0)),
            pl.BlockSpec((None, seq, DV), lambda b, h: (b, 0, zcol + h)),
        ],
        out_specs=pl.BlockSpec((None, seq, DV), lambda b, h: (b, 0, h)),
        out_shape=jax.ShapeDtypeStruct((bsz, seq, D_ATT), BF16),
        scratch_shapes=[
            pltpu.VMEM((DV, 2 * tq), BF16),
            pltpu.VMEM((1, 2 * tq), F32),
            pltpu.VMEM((1, 2 * tq), F32),
            pltpu.VMEM((DV, 2 * tq), F32),
        ],
        compiler_params=pltpu.CompilerParams(
            dimension_semantics=("arbitrary", "arbitrary"), vmem_limit_bytes=VMEM_LIMIT),
        name="diff_attention",
    )(lam_params, subln, proj3, proj3, vt, proj3)


def _ssm_param_kernel(ldt_ref, lre_ref, lim_ref, lrec_ref, limc_ref, bre_ref, bim_ref,
                      ctre_ref, ctim_ref, t_ref, w_ref, v_ref):
    h = pl.program_id(0) % 2
    dt = jnp.exp(ldt_ref[...])
    lr, li = lre_ref[...], lim_ref[...]

    def cis_pow(n, lr_, li_):
        mag = jnp.exp(n * (lr_ * dt))
        ang = n * (li_ * dt)
        return mag * jnp.cos(ang), mag * jnp.sin(ang)

    a_re, a_im = cis_pow(1.0, lr, li)
    nr, ni = a_re - 1.0, a_im
    den = lr * lr + li * li
    coef_re = (nr * lr + ni * li) / den
    coef_im = (ni * lr - nr * li) / den
    br, bi = bre_ref[...], bim_ref[...]
    bb_re = coef_re * br - coef_im * bi
    bb_im = coef_re * bi + coef_im * br

    n_rows = lax.broadcasted_iota(jnp.int32, (CHUNK, CW), 0).astype(F32)
    p_re, p_im = cis_pow(n_rows, lr, li)
    slot = lax.broadcasted_iota(jnp.int32, (GROUP, CW), 1) // STATE
    for s in range(CHUNK):
        r_re = p_re[CHUNK - 1 - s:CHUNK - s]
        r_im = p_im[CHUNK - 1 - s:CHUNK - s]
        w_re = bb_re * r_re - bb_im * r_im
        w_im = bb_re * r_im + bb_im * r_re
        val = jnp.where(slot == h, w_re, jnp.where(slot == h + 2, w_im, 0.0))
        w_ref[s * GROUP:(s + 1) * GROUP, :] = val.astype(BF16)

    lrc, lic = lrec_ref[...], limc_ref[...]
    n_lanes = lax.broadcasted_iota(jnp.int32, (STATE, 128), 1).astype(F32)
    q_re, q_im = cis_pow(n_lanes, lrc, lic)
    sel_n = lax.broadcasted_iota(jnp.int32, (128, CW), 0)
    sel_t = lax.broadcasted_iota(jnp.int32, (128, CW), 1) // GROUP
    sel_c = lax.broadcasted_iota(jnp.int32, (128, CW), 1) % GROUP
    e_t0 = (sel_n == sel_t).astype(F32)
    e_t1 = (sel_n == sel_t + 1).astype(F32)
    e_co = (sel_n == sel_c).astype(F32)

    def expand(a, e):
        return jnp.dot(a, e, preferred_element_type=F32, precision=HIGHEST)

    c_re = expand(ctre_ref[...], e_co)
    c_im = expand(ctim_ref[...], e_co)
    p0_re, p0_im = expand(q_re, e_t0), expand(q_im, e_t0)
    p1_re, p1_im = expand(q_re, e_t1), expand(q_im, e_t1)

    r_re = p0_re * c_re - p0_im * c_im
    r_im = p0_re * c_im + p0_im * c_re
    bb64_re, bb64_im = bb_re[:, :STATE], bb_im[:, :STATE]
    kall = (jnp.dot(bb64_re, r_re, preferred_element_type=F32, precision=HIGHEST)
            - jnp.dot(bb64_im, r_im, preferred_element_type=F32, precision=HIGHEST))
    lane = lax.broadcasted_iota(jnp.int32, (GROUP, CW), 1)
    for s in range(CHUNK):
        shifted = kall if s == 0 else pltpu.roll(kall, s * GROUP, axis=1)
        t_ref[s * GROUP:(s + 1) * GROUP, :] = jnp.where(lane >= s * GROUP, shifted, 0.0).astype(BF16)

    v_re = p1_re * c_re - p1_im * c_im
    v_im = -(p1_re * c_im + p1_im * c_re)
    v_ref[...] = jnp.zeros(v_ref.shape, BF16)
    v_ref[pl.ds(pl.multiple_of(h * STATE, STATE), STATE), :] = v_re.astype(BF16)
    v_ref[pl.ds(pl.multiple_of((h + 2) * STATE, STATE), STATE), :] = v_im.astype(BF16)


def _ssm_params(log_dt, lam_re, lam_im, b_re, b_im, c_re, c_im):
    tile4 = lambda a: jnp.tile(a, (1, 1, 4))
    ldt = log_dt.reshape(GROUPS, 1, 1)
    lre4 = tile4(lam_re.reshape(GROUPS, 1, STATE))
    lim4 = tile4(lam_im.reshape(GROUPS, 1, STATE))
    lrec = lam_re.reshape(GROUPS, STATE, 1)
    limc = lam_im.reshape(GROUPS, STATE, 1)
    bt_re = tile4(jnp.swapaxes(b_re, 1, 2))
    bt_im = tile4(jnp.swapaxes(b_im, 1, 2))
    pad = lambda a: jnp.pad(jnp.swapaxes(a, 1, 2), ((0, 0), (0, 0), (0, 128 - GROUP)))
    ct_re, ct_im = pad(c_re), pad(c_im)

    def spec(shape):
        return pl.BlockSpec((None,) + shape, lambda g: (g, 0, 0))

    return pl.pallas_call(
        _ssm_param_kernel,
        grid=(GROUPS,),
        in_specs=[spec((1, 1)), spec((1, CW)), spec((1, CW)), spec((STATE, 1)), spec((STATE, 1)),
                  spec((GROUP, CW)), spec((GROUP, CW)), spec((STATE, 128)), spec((STATE, 128))],
        out_specs=[
            pl.BlockSpec((None, CW, CW), lambda g: (g, 0, 0)),
            pl.BlockSpec((None, CW, CW), lambda g: (g // 2, g % 2, 0)),
            pl.BlockSpec((None, CW, CW), lambda g: (g // 2, 0, g % 2)),
        ],
        out_shape=[
            jax.ShapeDtypeStruct((GROUPS, CW, CW), BF16),
            jax.ShapeDtypeStruct((PAIRS, 2 * CW, CW), BF16),
            jax.ShapeDtypeStruct((PAIRS, CW, 2 * CW), BF16),
        ],
        compiler_params=pltpu.CompilerParams(dimension_semantics=("arbitrary",)),
        name="ssm_params",
    )(ldt, lre4, lim4, lrec, limc, bt_re, bt_im, ct_re, ct_im)


def _ssm_kernel(u_ref, t_ref, w_ref, v_ref, d_ref, ldt_ref, lre_ref, lim_ref, y_ref,
                s_ref, xr_ref, xi_ref, *, bsz):
    rows = u_ref.shape[1]

    @pl.when(pl.program_id(0) == 0)
    def _():
        xr_ref[...] = jnp.zeros(xr_ref.shape, F32)
        xi_ref[...] = jnp.zeros(xi_ref.shape, F32)

    def contrib(gp, c):
        g0 = 2 * gp
        u_pair = jnp.concatenate([u_ref[g0], u_ref[g0 + 1]], axis=1)
        s_ref[gp] = _dot(u_pair, w_ref[gp])
        return c

    lax.fori_loop(0, PAIRS, contrib, 0)

    dt = jnp.exp(ldt_ref[...])
    mag = jnp.exp(float(CHUNK) * (lre_ref[...] * dt))
    ang = float(CHUNK) * (lim_ref[...] * dt)
    a_re, a_im = mag * jnp.cos(ang), mag * jnp.sin(ang)
    sub = lax.broadcasted_iota(jnp.int32, (PAIRS, SUBLANES, 128), 1)

    def scan_tile(n, c):
        r0 = pl.multiple_of(n * SUBLANES, SUBLANES)
        tile = s_ref[:, pl.ds(r0, SUBLANES), :]
        s_re, s_im = tile[:, :, :128], tile[:, :, 128:]
        x_re, x_im = xr_ref[...], xi_ref[...]
        in_re, in_im = x_re, x_im
        for step in range(SUBLANES // bsz):
            if step:
                x_re = pltpu.roll(x_re, bsz, axis=1)
                x_im = pltpu.roll(x_im, bsz, axis=1)
                live = (sub >= step * bsz) & (sub < (step + 1) * bsz)
                in_re = jnp.where(live, x_re, in_re)
                in_im = jnp.where(live, x_im, in_im)
            x_re, x_im = (a_re * x_re - a_im * x_im + s_re,
                          a_re * x_im + a_im * x_re + s_im)
        s_ref[:, pl.ds(r0, SUBLANES), :] = jnp.concatenate([in_re, in_im], axis=2)
        if SUBLANES // bsz > 1:
            x_re = pltpu.roll(x_re, bsz, axis=1)
            x_im = pltpu.roll(x_im, bsz, axis=1)
        xr_ref[...] = x_re
        xi_ref[...] = x_im
        return c

    lax.fori_loop(0, rows // SUBLANES, scan_tile, 0)

    def emit(gp, c):
        g0 = 2 * gp
        y_state = _dot(s_ref[gp].astype(BF16), v_ref[gp])
        for hh in range(2):
            u = u_ref[g0 + hh]
            y = _dot(u, t_ref[g0 + hh]) + y_state[:, hh * CW:(hh + 1) * CW] + d_ref[g0 + hh] * u.astype(F32)
            y_ref[g0 + hh] = jax.nn.gelu(y).astype(BF16)
        return c

    lax.fori_loop(0, PAIRS, emit, 0)


def _ssm(u3, t_mat, w_mat, v_mat, d_chunk, ldt_p, lre_p, lim_p, bsz, rows_per_step):
    total_rows = u3.shape[1]
    kern = functools.partial(_ssm_kernel, bsz=bsz)
    const3 = lambda t: (0, 0, 0)
    once = pl.Buffered(1)
    return pl.pallas_call(
        kern,
        grid=(total_rows // rows_per_step,),
        in_specs=[
            pl.BlockSpec((GROUPS, rows_per_step, CW), lambda t: (0, t, 0)),
            pl.BlockSpec((GROUPS, CW, CW), const3, pipeline_mode=once),
            pl.BlockSpec((PAIRS, 2 * CW, CW), const3, pipeline_mode=once),
            pl.BlockSpec((PAIRS, CW, 2 * CW), const3, pipeline_mode=once),
            pl.BlockSpec((GROUPS, 1, CW), const3, pipeline_mode=once),
            pl.BlockSpec((PAIRS, 1, 128), const3, pipeline_mode=once),
            pl.BlockSpec((PAIRS, 1, 128), const3, pipeline_mode=once),
            pl.BlockSpec((PAIRS, 1, 128), const3, pipeline_mode=once),
        ],
        out_specs=pl.BlockSpec((GROUPS, rows_per_step, CW), lambda t: (0, t, 0)),
        out_shape=jax.ShapeDtypeStruct((GROUPS, total_rows, CW), BF16),
        scratch_shapes=[
            pltpu.VMEM((PAIRS, rows_per_step, 2 * 128), F32),
            pltpu.VMEM((PAIRS, SUBLANES, 128), F32),
            pltpu.VMEM((PAIRS, SUBLANES, 128), F32),
        ],
        compiler_params=pltpu.CompilerParams(
            dimension_semantics=("arbitrary",), vmem_limit_bytes=VMEM_LIMIT),
        name="ssm_scan",
    )(u3, t_mat, w_mat, v_mat, d_chunk, ldt_p, lre_p, lim_p)


def _merge_kernel(x_ref, ya_ref, yg_ref, zs_ref, ga_ref, gs_ref, wg_ref, bg_ref,
                  wpa_ref, wps_ref, wo_ref, o_ref):
    yg = yg_ref[...]
    glu = _dot(yg, wg_ref[...]) + bg_ref[...]
    y_ssm = (yg.astype(F32) * jax.nn.sigmoid(glu) * zs_ref[...].astype(F32)).astype(BF16)
    merged = (ga_ref[...].astype(F32) * _dot(ya_ref[...], wpa_ref[...])
              + gs_ref[...].astype(F32) * _dot(y_ssm, wps_ref[...]))
    o_ref[...] = x_ref[...] + _dot(merged.astype(BF16), wo_ref[...])


def _merge(x2, y_att, yg, proj, w_glu, b_glu, w_pa, w_ps, w_out, tm):
    tokens = x2.shape[0]
    const = lambda i: (0, 0)
    once = pl.Buffered(1)
    return pl.pallas_call(
        _merge_kernel,
        grid=(tokens // tm,),
        in_specs=[
            pl.BlockSpec((tm, D_MODEL), lambda i: (i, 0)),
            pl.BlockSpec((tm, D_ATT), lambda i: (i, 0)),
            pl.BlockSpec((tm, D_SSM), lambda i: (i, 0)),
            pl.BlockSpec((tm, D_SSM), lambda i: (i, TILE_ZSSM)),
            pl.BlockSpec((tm, D_MODEL), lambda i: (i, TILE_GATT0 // 2)),
            pl.BlockSpec((tm, D_MODEL), lambda i: (i, TILE_GSSM0 // 2)),
            pl.BlockSpec((D_SSM, D_SSM), const, pipeline_mode=once),
            pl.BlockSpec((1, D_SSM), const, pipeline_mode=once),
            pl.BlockSpec((D_ATT, D_MODEL), const, pipeline_mode=once),
            pl.BlockSpec((D_SSM, D_MODEL), const, pipeline_mode=once),
            pl.BlockSpec((D_MODEL, D_MODEL), const, pipeline_mode=once),
        ],
        out_specs=pl.BlockSpec((tm, D_MODEL), lambda i: (i, 0)),
        out_shape=jax.ShapeDtypeStruct((tokens, D_MODEL), F32),
        compiler_params=pltpu.CompilerParams(
            dimension_semantics=("arbitrary",), vmem_limit_bytes=VMEM_LIMIT),
        name="merge_out",
    )(x2, y_att, yg, proj, proj, proj, w_glu, b_glu, w_pa, w_ps, w_out)


def kernel(x, ln_gain, w_in, q_norm_gain, k_norm_gain, lambda_q1, lambda_k1, lambda_q2, lambda_k2,
           subln_gain, ssm_lambda_re, ssm_lambda_im, ssm_log_dt, ssm_b_re, ssm_b_im, ssm_c_re,
           ssm_c_im, ssm_d, w_glu, b_glu, w_proj_att, w_proj_ssm, w_out):
    bsz, seq, _ = x.shape
    assert ln_gain.shape[0] == 1 and x.shape[2] == D_MODEL and w_in.shape[2] == N_IN
    assert SUBLANES % bsz == 0 and seq % (CHUNK * SUBLANES) == 0
    tokens = bsz * seq
    n_chunks = seq // CHUNK
    tm_in = min(1024, seq)
    tq = min(512, seq)
    tm_out = min(256, seq)
    rows_per_step = min(32, n_chunks) * bsz

    x2 = x.reshape(tokens, D_MODEL)
    scale = DQK ** -0.5
    q_gain = (jnp.tile(q_norm_gain[0], 2 * HEADS) * scale).reshape(1, 1024)
    k_gain = jnp.tile(k_norm_gain[0], 2 * HEADS).reshape(1, 1024)
    proj, vt = _in_proj(x2, ln_gain[0].reshape(1, D_MODEL), w_in[0].astype(BF16), q_gain, k_gain,
                        bsz, seq, tm_in)

    lam_params = jnp.stack([lambda_q1[0], lambda_k1[0], lambda_q2[0], lambda_k2[0]])
    subln = (subln_gain[0] * (1.0 - LAMBDA_INIT)).reshape(1, DV)
    score_bound = 1.02 * 8.0 * jnp.max(jnp.abs(q_norm_gain[0] * k_norm_gain[0]))
    attend = functools.partial(_attention, bsz=bsz, seq=seq, tq=tq)
    y_att = lax.cond(score_bound <= SCORE_BOUND_FAST,
                     functools.partial(attend, fast=True), functools.partial(attend, fast=False),
                     lam_params, subln, proj.reshape(bsz, seq, N_IN), vt)

    t_mat, w_mat, v_mat = _ssm_params(ssm_log_dt[0], ssm_lambda_re[0], ssm_lambda_im[0],
                                      ssm_b_re[0], ssm_b_im[0], ssm_c_re[0], ssm_c_im[0])
    u = proj[:, TILE_U * 1024:(TILE_U + 1) * 1024]
    u3 = (u.reshape(bsz, n_chunks, CHUNK, GROUPS, GROUP).transpose(3, 1, 0, 2, 4)
          .reshape(GROUPS, n_chunks * bsz, CW))
    d_chunk = jnp.tile(ssm_d[0].reshape(GROUPS, 1, GROUP), (1, 1, CHUNK))
    ldt_p = jnp.repeat(ssm_log_dt[0], STATE).reshape(PAIRS, 1, 128)
    lre_p = ssm_lambda_re[0].reshape(PAIRS, 1, 128)
    lim_p = ssm_lambda_im[0].reshape(PAIRS, 1, 128)
    yg3 = _ssm(u3, t_mat, w_mat, v_mat, d_chunk, ldt_p, lre_p, lim_p, bsz, rows_per_step)
    yg = (yg3.reshape(GROUPS, n_chunks, bsz, CHUNK, GROUP).transpose(2, 1, 3, 0, 4)
          .reshape(tokens, D_SSM))

    out = _merge(x2, y_att.reshape(tokens, D_ATT), yg, proj, w_glu[0].astype(BF16),
                 b_glu[0].reshape(1, D_SSM), w_proj_att[0].astype(BF16), w_proj_ssm[0].astype(BF16),
                 w_out[0].astype(BF16), tm_out)
    return out.reshape(bsz, seq, D_MODEL)
```

```python
import functools
import math

import jax
import jax.numpy as jnp
from jax import lax
from jax.experimental import pallas as pl
from jax.experimental.pallas import tpu as pltpu

F32 = jnp.float32
BF16 = jnp.bfloat16
HIGHEST = lax.Precision.HIGHEST

D_MODEL = 2048
HEADS = 8
DQK = 64
DV = 2 * DQK
D_ATT = HEADS * DV
D_SSM = 1024
GROUP = 16
GROUPS = D_SSM // GROUP
PAIRS = GROUPS // 2
STATE = 64
N_IN = 6 * 1024 + 2 * D_MODEL
RMS_EPS = 1e-6
LAMBDA_INIT = 0.8 - 0.6 * math.exp(-0.3 * 0)
CHUNK = 16
CW = CHUNK * GROUP
SSM_SPLIT = 4
SG = GROUPS // SSM_SPLIT
SP = SG // 2
NEG = -1e30
SCORE_BOUND_FAST = 30.0
KV_GROUP = 4

TILE_Q, TILE_K, TILE_V, TILE_ZATT, TILE_U, TILE_ZSSM = 0, 1, 2, 3, 4, 5
TILE_GATT0, TILE_GSSM0 = 6, 8
N_TILES = N_IN // 1024

VMEM_LIMIT = 56 * 1024 * 1024


def _dot(a, b):
    return jnp.dot(a, b, preferred_element_type=F32)


def _in_proj_kernel(x_ref, ln_ref, w_ref, qg_ref, kg_ref, gsum_ref, proj_ref, vt_ref, h_ref):
    j = pl.program_id(1)

    @pl.when(j == 0)
    def _():
        x = x_ref[...]
        ms = jnp.mean(x * x, axis=-1, keepdims=True)
        h_ref[...] = (x * lax.rsqrt(ms + RMS_EPS) * ln_ref[...]).astype(BF16)

    acc = _dot(h_ref[...], w_ref[...])

    def group_rms_norm(gain):
        sq = acc * acc
        hi = sq.astype(BF16)
        lo = (sq - hi.astype(F32)).astype(BF16)
        g = gsum_ref[...]
        cols = []
        for c in range(4):
            sl = slice(c * 256, (c + 1) * 256)
            cols.append(_dot(hi[:, sl], g) + _dot(lo[:, sl], g))
        ms = jnp.concatenate(cols, axis=1) * (1.0 / DQK)
        return acc * lax.rsqrt(ms + RMS_EPS) * gain

    @pl.when(j == TILE_Q)
    def _():
        proj_ref[...] = group_rms_norm(qg_ref[...]).astype(BF16)

    @pl.when(j == TILE_K)
    def _():
        proj_ref[...] = group_rms_norm(kg_ref[...]).astype(BF16)

    @pl.when(j == TILE_V)
    def _():
        proj_ref[...] = acc.astype(BF16)
        vt_ref[...] = acc.T.astype(BF16)

    @pl.when(j == TILE_U)
    def _():
        proj_ref[...] = acc.astype(BF16)

    @pl.when((j == TILE_ZATT) | (j == TILE_ZSSM))
    def _():
        proj_ref[...] = jax.nn.silu(acc).astype(BF16)

    @pl.when(j >= TILE_GATT0)
    def _():
        proj_ref[...] = jax.nn.sigmoid(acc).astype(BF16)


def _in_proj(x2, ln_gain, w_in, q_gain, k_gain, bsz, seq, tm):
    tokens = bsz * seq
    tiles_per_seq = seq // tm
    gi = lax.broadcasted_iota(jnp.int32, (256, 256), 0) // DQK
    gj = lax.broadcasted_iota(jnp.int32, (256, 256), 1) // DQK
    gsum = (gi == gj).astype(BF16)
    return pl.pallas_call(
        _in_proj_kernel,
        grid=(tokens // tm, N_TILES),
        in_specs=[
            pl.BlockSpec((tm, D_MODEL), lambda i, j: (i, 0)),
            pl.BlockSpec((1, D_MODEL), lambda i, j: (0, 0)),
            pl.BlockSpec((D_MODEL, 1024), lambda i, j: (0, j)),
            pl.BlockSpec((1, 1024), lambda i, j: (0, 0)),
            pl.BlockSpec((1, 1024), lambda i, j: (0, 0)),
            pl.BlockSpec((256, 256), lambda i, j: (0, 0)),
        ],
        out_specs=[
            pl.BlockSpec((tm, 1024), lambda i, j: (i, j)),
            pl.BlockSpec((None, D_ATT, tm), lambda i, j: (i // tiles_per_seq, 0, i % tiles_per_seq)),
        ],
        out_shape=[
            jax.ShapeDtypeStruct((tokens, N_IN), BF16),
            jax.ShapeDtypeStruct((bsz, D_ATT, seq), BF16),
        ],
        scratch_shapes=[pltpu.VMEM((tm, D_MODEL), BF16)],
        compiler_params=pltpu.CompilerParams(
            dimension_semantics=("arbitrary", "arbitrary"), vmem_limit_bytes=VMEM_LIMIT),
        name="in_proj",
    )(x2, ln_gain, w_in, q_gain, k_gain, gsum)


def _attn_kernel(lp_ref, sg_ref, q_ref, k_ref, vt_ref, z_ref, o_ref,
                 qbd_ref, m_ref, l_ref, acc_ref, *, seq, tq, fast):
    lp = lp_ref[...]
    lam = (jnp.exp(jnp.sum(lp[0:1] * lp[1:2], axis=-1, keepdims=True))
           - jnp.exp(jnp.sum(lp[2:3] * lp[3:4], axis=-1, keepdims=True)) + LAMBDA_INIT)

    def q_tile(i, carry):
        q0 = pl.multiple_of(i * tq, tq)
        qt = q_ref[pl.ds(q0, tq), :].astype(F32).T
        row = lax.broadcasted_iota(jnp.int32, (DV, tq), 0)
        zero = jnp.zeros_like(qt)
        qbd_ref[:, :tq] = jnp.where(row < DQK, qt, zero).astype(BF16)
        qbd_ref[:, tq:] = jnp.where(row >= DQK, qt, zero).astype(BF16)
        m_ref[...] = jnp.full(m_ref.shape, NEG, F32)
        l_ref[...] = jnp.zeros(l_ref.shape, F32)
        acc_ref[...] = jnp.zeros(acc_ref.shape, F32)

        def block(j, masked, state):
            m, l, acc = state
            k0 = pl.multiple_of(j * tq, tq)
            s = _dot(k_ref[pl.ds(k0, tq), :], qbd_ref[...])
            if masked:
                kpos = lax.broadcasted_iota(jnp.int32, (tq, 2 * tq), 0)
                qpos = lax.broadcasted_iota(jnp.int32, (tq, 2 * tq), 1)
                qpos = jnp.where(qpos >= tq, qpos - tq, qpos)
                s = jnp.where(kpos <= qpos, s, NEG)
            vt = vt_ref[:, pl.ds(k0, tq)]
            if fast:
                p = jnp.exp(s)
                return m, l + jnp.sum(p, axis=0, keepdims=True), acc + _dot(vt, p.astype(BF16))
            m_new = jnp.maximum(m, jnp.max(s, axis=0, keepdims=True))
            alpha = jnp.exp(m - m_new)
            p = jnp.exp(s - m_new)
            return (m_new, alpha * l + jnp.sum(p, axis=0, keepdims=True),
                    alpha * acc + _dot(vt, p.astype(BF16)))

        def load_state():
            return m_ref[...], l_ref[...], acc_ref[...]

        def group(jj, c):
            state = load_state()
            for u in range(KV_GROUP):
                state = block(jj * KV_GROUP + u, False, state)
            m_ref[...], l_ref[...], acc_ref[...] = state
            return c

        n_groups = lax.shift_right_logical(i, KV_GROUP.bit_length() - 1)
        lax.fori_loop(0, n_groups, group, 0)

        for rem in range(KV_GROUP):
            @pl.when((i & (KV_GROUP - 1)) == rem)
            def _():
                state = load_state()
                for u in range(rem):
                    state = block(n_groups * KV_GROUP + u, False, state)
                _, l, acc = block(i, True, state)
                o = acc * (1.0 / l)
                a = o[:, :tq] - lam * o[:, tq:]
                ms = jnp.mean(a * a, axis=0, keepdims=True)
                n = (a * lax.rsqrt(ms + RMS_EPS)).T
                out = n * sg_ref[...] * z_ref[pl.ds(q0, tq), :].astype(F32)
                o_ref[pl.ds(q0, tq), :] = out.astype(BF16)
        return carry

    lax.fori_loop(0, seq // tq, q_tile, 0)


def _attention(lam_params, subln, proj3, vt, *, bsz, seq, tq, fast):
    kern = functools.partial(_attn_kernel, seq=seq, tq=tq, fast=fast)
    qcol, kcol, zcol = TILE_Q * HEADS, TILE_K * HEADS, TILE_ZATT * HEADS
    return pl.pallas_call(
        kern,
        grid=(bsz, HEADS),
        in_specs=[
            pl.BlockSpec((4, DQK), lambda b, h: (0, 0)),
            pl.BlockSpec((1, DV), lambda b, h: (0, 0)),
            pl.BlockSpec((None, seq, DV), lambda b, h: (b, 0, qcol + h)),
            pl.BlockSpec((None, seq, DV), lambda b, h: (b, 0, kcol + h)),
            pl.BlockSpec((None, DV, seq), lambda b, h: (b, h, 0)),
            pl.BlockSpec((None, seq, DV), lambda b, h: (b, 0, zcol + h)),
        ],
        out_specs=pl.BlockSpec((None, seq, DV), lambda b, h: (b, 0, h)),
        out_shape=jax.ShapeDtypeStruct((bsz, seq, D_ATT), BF16),
        scratch_shapes=[
            pltpu.VMEM((DV, 2 * tq), BF16),
            pltpu.VMEM((1, 2 * tq), F32),
            pltpu.VMEM((1, 2 * tq), F32),
            pltpu.VMEM((DV, 2 * tq), F32),
        ],
        compiler_params=pltpu.CompilerParams(
            dimension_semantics=("arbitrary", "arbitrary"), vmem_limit_bytes=VMEM_LIMIT),
        name="diff_attention",
    )(lam_params, subln, proj3, proj3, vt, proj3)


def _ssm_param_kernel(ldt_ref, lre_ref, lim_ref, lrec_ref, limc_ref, btre_ref, btim_ref,
                      c4re_ref, c4im_ref, bpre_ref, bpim_ref, cre_ref, cim_ref,
                      tt_ref, w_ref, vt_ref):
    h = pl.program_id(0) % 2
    dt = jnp.exp(ldt_ref[...])

    def cis_pow(n, lr_, li_):
        mag = jnp.exp(n * (lr_ * dt))
        ang = n * (li_ * dt)
        return mag * jnp.cos(ang), mag * jnp.sin(ang)

    def zoh(lr_, li_):
        a_re, a_im = cis_pow(1.0, lr_, li_)
        nr, ni = a_re - 1.0, a_im
        den = lr_ * lr_ + li_ * li_
        return (nr * lr_ + ni * li_) / den, (ni * lr_ - nr * li_) / den

    def complex_mul(x_re, x_im, y_re, y_im):
        return x_re * y_re - x_im * y_im, x_re * y_im + x_im * y_re

    lr, li = lre_ref[...], lim_ref[...]
    coef_re, coef_im = zoh(lr, li)
    bb_re, bb_im = complex_mul(coef_re, coef_im, btre_ref[...], btim_ref[...])
    n_rows = jnp.minimum(lax.broadcasted_iota(jnp.int32, (24, CW), 0), CHUNK).astype(F32)
    p_re, p_im = cis_pow(n_rows, lr, li)
    c4_re, c4_im = c4re_ref[...], c4im_ref[...]
    slot = lax.broadcasted_iota(jnp.int32, (GROUP, CW), 1) // STATE

    def to_slots(v_re, v_im):
        return jnp.where(slot == h, v_re, jnp.where(slot == h + 2, v_im, 0.0)).astype(BF16)

    for s in range(CHUNK):
        n = CHUNK - 1 - s
        w_re, w_im = complex_mul(bb_re, bb_im, p_re[n:n + 1], p_im[n:n + 1])
        w_ref[s * GROUP:(s + 1) * GROUP, :] = to_slots(w_re, w_im)
        v_re, v_im = complex_mul(c4_re, c4_im, p_re[s + 1:s + 2], p_im[s + 1:s + 2])
        vt_ref[s * GROUP:(s + 1) * GROUP, :] = to_slots(v_re, -v_im)

    lrc, lic = lrec_ref[...], limc_ref[...]
    cc_re, cc_im = zoh(lrc, lic)
    bp_re, bp_im = complex_mul(cc_re, cc_im, bpre_ref[...], bpim_ref[...])
    n_lanes = jnp.minimum(lax.broadcasted_iota(jnp.int32, (STATE, 128), 1), CHUNK).astype(F32)
    q_re, q_im = cis_pow(n_lanes, lrc, lic)
    sel = lax.broadcasted_iota(jnp.int32, (128, CW), 0)
    lane_r = lax.broadcasted_iota(jnp.int32, (128, CW), 1) // GROUP
    lane_ci = lax.broadcasted_iota(jnp.int32, (128, CW), 1) % GROUP
    e_pow = (sel == CHUNK - 1 - lane_r).astype(F32)
    e_ci = (sel == lane_ci).astype(F32)

    def hdot(a, b):
        return jnp.dot(a, b, preferred_element_type=F32, precision=HIGHEST)

    r_re, r_im = complex_mul(hdot(q_re, e_pow), hdot(q_im, e_pow), hdot(bp_re, e_ci), hdot(bp_im, e_ci))
    krev = hdot(cre_ref[...], r_re) - hdot(cim_ref[...], r_im)
    lane_s = lax.broadcasted_iota(jnp.int32, (GROUP, CW), 1) // GROUP
    for t in range(CHUNK):
        shift = (CHUNK - 1 - t) * GROUP
        moved = krev if shift == 0 else pltpu.roll(krev, CW - shift, axis=1)
        tt_ref[t * GROUP:(t + 1) * GROUP, :] = jnp.where(lane_s <= t, moved, 0.0).astype(BF16)


def _ssm_params(log_dt, lam_re, lam_im, b_re, b_im, c_re, c_im):
    tile4 = lambda a: jnp.tile(a, (1, 1, 4))
    pad128 = lambda a: jnp.pad(a, ((0, 0), (0, 0), (0, 128 - GROUP)))
    ldt = log_dt.reshape(GROUPS, 1, 1)
    lre4 = tile4(lam_re.reshape(GROUPS, 1, STATE))
    lim4 = tile4(lam_im.reshape(GROUPS, 1, STATE))
    lrec = lam_re.reshape(GROUPS, STATE, 1)
    limc = lam_im.reshape(GROUPS, STATE, 1)
    bt_re = tile4(jnp.swapaxes(b_re, 1, 2))
    bt_im = tile4(jnp.swapaxes(b_im, 1, 2))
    c4_re, c4_im = tile4(c_re), tile4(c_im)
    bp_re, bp_im = pad128(b_re), pad128(b_im)

    def spec(shape):
        return pl.BlockSpec((None,) + shape, lambda g: (g, 0, 0))

    return pl.pallas_call(
        _ssm_param_kernel,
        grid=(GROUPS,),
        in_specs=[spec((1, 1)), spec((1, CW)), spec((1, CW)), spec((STATE, 1)), spec((STATE, 1)),
                  spec((GROUP, CW)), spec((GROUP, CW)), spec((GROUP, CW)), spec((GROUP, CW)),
                  spec((STATE, 128)), spec((STATE, 128)), spec((GROUP, STATE)), spec((GROUP, STATE))],
        out_specs=[
            pl.BlockSpec((None, CW, CW), lambda g: (g, 0, 0)),
            pl.BlockSpec((None, CW, CW), lambda g: (g // 2, g % 2, 0)),
            pl.BlockSpec((None, CW, CW), lambda g: (g, 0, 0)),
        ],
        out_shape=[
            jax.ShapeDtypeStruct((GROUPS, CW, CW), BF16),
            jax.ShapeDtypeStruct((PAIRS, 2 * CW, CW), BF16),
            jax.ShapeDtypeStruct((GROUPS, CW, CW), BF16),
        ],
        compiler_params=pltpu.CompilerParams(dimension_semantics=("arbitrary",)),
        name="ssm_params",
    )(ldt, lre4, lim4, lrec, limc, bt_re, bt_im, c4_re, c4_im, bp_re, bp_im, c_re, c_im)


def _ssm_kernel(u_ref, tt_ref, w_ref, vt_ref, d_ref, are_ref, aim_ref, y_ref,
                uf_ref, lt_ref, sre_ref, sim_ref, yt_ref, st_ref, xr_ref, xi_ref, *, bsz, ct):
    cols = bsz * ct
    halves = [slice(0, 128), slice(128, 256)]

    @pl.when(pl.program_id(1) == 0)
    def _():
        xr_ref[...] = jnp.zeros(xr_ref.shape, F32)
        xi_ref[...] = jnp.zeros(xi_ref.shape, F32)

    for b in range(bsz):
        u = u_ref[b].astype(F32)
        for hf, sl in enumerate(halves):
            uf_ref[b, hf] = u[:, sl]
    for s in range(CHUNK):
        x_s = jnp.concatenate(
            [jnp.concatenate([uf_ref[b, hf, pl.ds(s, ct, stride=CHUNK), :] for b in range(bsz)], axis=0)
             for hf in range(2)], axis=1)
        lt_ref[:, s * GROUP:(s + 1) * GROUP, :] = x_s.T.astype(BF16).reshape(SG, GROUP, cols)

    def pair_rows(gp):
        return pl.ds(pl.multiple_of(gp * cols, cols), cols)

    def contrib(gp, c):
        lt_pair = lt_ref[pl.ds(2 * gp, 2)].reshape(2 * CW, cols)
        s = lax.dot_general(lt_pair, w_ref[gp], (((0,), (0,)), ((), ())), preferred_element_type=F32)
        sre_ref[pair_rows(gp), :] = s[:, :128]
        sim_ref[pair_rows(gp), :] = s[:, 128:]
        return c

    lax.fori_loop(0, SP, contrib, 0)

    a_re, a_im = are_ref[...], aim_ref[...]

    def scan_step(n, c):
        rows = pl.ds(n, SP * bsz, stride=ct)
        s_re, s_im = sre_ref[rows, :], sim_ref[rows, :]
        x_re, x_im = xr_ref[...], xi_ref[...]
        sre_ref[rows, :] = x_re
        sim_ref[rows, :] = x_im
        xr_ref[...] = a_re * x_re - a_im * x_im + s_re
        xi_ref[...] = a_re * x_im + a_im * x_re + s_im
        return c

    lax.fori_loop(0, ct, scan_step, 0)

    def emit(gp, c):
        x_in = jnp.concatenate([sre_ref[pair_rows(gp), :], sim_ref[pair_rows(gp), :]],
                               axis=1).astype(BF16)
        for hh in range(2):
            g = 2 * gp + hh
            yt_ref[g] = _dot(tt_ref[g], lt_ref[g]) + lax.dot_general(
                vt_ref[g], x_in, (((1,), (1,)), ((), ())), preferred_element_type=F32)
        return c

    lax.fori_loop(0, SP, emit, 0)

    for t in range(CHUNK):
        z = yt_ref[:, t * GROUP:(t + 1) * GROUP, :].reshape(SG * GROUP, cols).T
        for b in range(bsz):
            for hf, sl in enumerate(halves):
                st_ref[b, hf, pl.ds(t, ct, stride=CHUNK), :] = z[b * ct:(b + 1) * ct, sl]
    for b in range(bsz):
        pre = jnp.concatenate([st_ref[b, 0], st_ref[b, 1]], axis=1)
        u = jnp.concatenate([uf_ref[b, 0], uf_ref[b, 1]], axis=1)
        y_ref[b] = jax.nn.gelu(pre + d_ref[...] * u).astype(BF16)


def _ssm(proj3, tt_mat, w_mat, vt_mat, d_skip, a_re, a_im, bsz, seq, ct):
    tok = ct * CHUNK
    cols = bsz * ct
    kern = functools.partial(_ssm_kernel, bsz=bsz, ct=ct)
    ucol = TILE_U * SSM_SPLIT
    return pl.pallas_call(
        kern,
        grid=(SSM_SPLIT, seq // tok),
        in_specs=[
            pl.BlockSpec((bsz, tok, CW), lambda q, t: (0, t, ucol + q)),
            pl.BlockSpec((SG, CW, CW), lambda q, t: (q, 0, 0)),
            pl.BlockSpec((SP, 2 * CW, CW), lambda q, t: (q, 0, 0)),
            pl.BlockSpec((SG, CW, CW), lambda q, t: (q, 0, 0)),
            pl.BlockSpec((1, CW), lambda q, t: (0, q)),
            pl.BlockSpec((None, SP * bsz, 128), lambda q, t: (q, 0, 0)),
            pl.BlockSpec((None, SP * bsz, 128), lambda q, t: (q, 0, 0)),
        ],
        out_specs=pl.BlockSpec((bsz, tok, CW), lambda q, t: (0, t, q)),
        out_shape=jax.ShapeDtypeStruct((bsz, seq, D_SSM), BF16),
        scratch_shapes=[
            pltpu.VMEM((bsz, 2, tok, 128), F32),
            pltpu.VMEM((SG, CW, cols), BF16),
            pltpu.VMEM((SP * cols, 128), F32),
            pltpu.VMEM((SP * cols, 128), F32),
            pltpu.VMEM((SG, CW, cols), F32),
            pltpu.VMEM((bsz, 2, tok, 128), F32),
            pltpu.VMEM((SP * bsz, 128), F32),
            pltpu.VMEM((SP * bsz, 128), F32),
        ],
        compiler_params=pltpu.CompilerParams(
            dimension_semantics=("arbitrary", "arbitrary"), vmem_limit_bytes=VMEM_LIMIT),
        name="ssm_scan",
    )(proj3, tt_mat, w_mat, vt_mat, d_skip, a_re, a_im)


def _chunk_decay_kernel(ldt_ref, lre_ref, lim_ref, are_ref, aim_ref):
    dt = jnp.exp(ldt_ref[...])
    mag = jnp.exp(float(CHUNK) * (lre_ref[...] * dt))
    ang = float(CHUNK) * (lim_ref[...] * dt)
    are_ref[...] = mag * jnp.cos(ang)
    aim_ref[...] = mag * jnp.sin(ang)


def _chunk_decay(ldt_p, lre_p, lim_p):
    shape = jax.ShapeDtypeStruct(ldt_p.shape, F32)
    return pl.pallas_call(_chunk_decay_kernel, out_shape=[shape, shape], name="ssm_chunk_decay")(
        ldt_p, lre_p, lim_p)


def _merge_kernel(x_ref, ya_ref, yg_ref, zs_ref, ga_ref, gs_ref, wg_ref, bg_ref,
                  wpa_ref, wps_ref, wo_ref, o_ref):
    yg = yg_ref[...]
    glu = _dot(yg, wg_ref[...]) + bg_ref[...]
    y_ssm = (yg.astype(F32) * jax.nn.sigmoid(glu) * zs_ref[...].astype(F32)).astype(BF16)
    merged = (ga_ref[...].astype(F32) * _dot(ya_ref[...], wpa_ref[...])
              + gs_ref[...].astype(F32) * _dot(y_ssm, wps_ref[...]))
    o_ref[...] = x_ref[...] + _dot(merged.astype(BF16), wo_ref[...])


def _merge(x2, y_att, yg, proj, w_glu, b_glu, w_pa, w_ps, w_out, tm):
    tokens = x2.shape[0]
    const = lambda i: (0, 0)
    once = pl.Buffered(1)
    return pl.pallas_call(
        _merge_kernel,
        grid=(tokens // tm,),
        in_specs=[
            pl.BlockSpec((tm, D_MODEL), lambda i: (i, 0)),
            pl.BlockSpec((tm, D_ATT), lambda i: (i, 0)),
            pl.BlockSpec((tm, D_SSM), lambda i: (i, 0)),
            pl.BlockSpec((tm, D_SSM), lambda i: (i, TILE_ZSSM)),
            pl.BlockSpec((tm, D_MODEL), lambda i: (i, TILE_GATT0 // 2)),
            pl.BlockSpec((tm, D_MODEL), lambda i: (i, TILE_GSSM0 // 2)),
            pl.BlockSpec((D_SSM, D_SSM), const, pipeline_mode=once),
            pl.BlockSpec((1, D_SSM), const, pipeline_mode=once),
            pl.BlockSpec((D_ATT, D_MODEL), const, pipeline_mode=once),
            pl.BlockSpec((D_SSM, D_MODEL), const, pipeline_mode=once),
            pl.BlockSpec((D_MODEL, D_MODEL), const, pipeline_mode=once),
        ],
        out_specs=pl.BlockSpec((tm, D_MODEL), lambda i: (i, 0)),
        out_shape=jax.ShapeDtypeStruct((tokens, D_MODEL), F32),
        compiler_params=pltpu.CompilerParams(
            dimension_semantics=("arbitrary",), vmem_limit_bytes=VMEM_LIMIT),
        name="merge_out",
    )(x2, y_att, yg, proj, proj, proj, w_glu, b_glu, w_pa, w_ps, w_out)


def kernel(x, ln_gain, w_in, q_norm_gain, k_norm_gain, lambda_q1, lambda_k1, lambda_q2, lambda_k2,
           subln_gain, ssm_lambda_re, ssm_lambda_im, ssm_log_dt, ssm_b_re, ssm_b_im, ssm_c_re,
           ssm_c_im, ssm_d, w_glu, b_glu, w_proj_att, w_proj_ssm, w_out):
    bsz, seq, _ = x.shape
    assert ln_gain.shape[0] == 1 and x.shape[2] == D_MODEL and w_in.shape[2] == N_IN
    tokens = bsz * seq
    tm_in = min(1024, seq)
    tq = min(512, seq)
    tm_out = min(256, seq)
    ct = 128 // bsz
    assert 128 % bsz == 0 and ct % 8 == 0 and seq % (ct * CHUNK) == 0

    x2 = x.reshape(tokens, D_MODEL)
    scale = DQK ** -0.5
    q_gain = (jnp.tile(q_norm_gain[0], 2 * HEADS) * scale).reshape(1, 1024)
    k_gain = jnp.tile(k_norm_gain[0], 2 * HEADS).reshape(1, 1024)
    proj, vt = _in_proj(x2, ln_gain[0].reshape(1, D_MODEL), w_in[0].astype(BF16), q_gain, k_gain,
                        bsz, seq, tm_in)
    proj3 = proj.reshape(bsz, seq, N_IN)

    lam_params = jnp.stack([lambda_q1[0], lambda_k1[0], lambda_q2[0], lambda_k2[0]])
    subln = (subln_gain[0] * (1.0 - LAMBDA_INIT)).reshape(1, DV)
    score_bound = 1.02 * 8.0 * jnp.max(jnp.abs(q_norm_gain[0] * k_norm_gain[0]))
    attend = functools.partial(_attention, bsz=bsz, seq=seq, tq=tq)
    y_att = lax.cond(score_bound <= SCORE_BOUND_FAST,
                     functools.partial(attend, fast=True), functools.partial(attend, fast=False),
                     lam_params, subln, proj3, vt)

    tt_mat, w_mat, vt_mat = _ssm_params(ssm_log_dt[0], ssm_lambda_re[0], ssm_lambda_im[0],
                                        ssm_b_re[0], ssm_b_im[0], ssm_c_re[0], ssm_c_im[0])
    per_row = lambda a: jnp.repeat(a.reshape(PAIRS, 128), bsz, axis=0).reshape(SSM_SPLIT, SP * bsz, 128)
    a_re, a_im = _chunk_decay(per_row(jnp.repeat(ssm_log_dt[0], STATE)), per_row(ssm_lambda_re[0]),
                              per_row(ssm_lambda_im[0]))
    yg = _ssm(proj3, tt_mat, w_mat, vt_mat, ssm_d[0].reshape(1, D_SSM), a_re, a_im, bsz, seq, ct)

    out = _merge(x2, y_att.reshape(tokens, D_ATT), yg.reshape(tokens, D_SSM), proj, w_glu[0].astype(BF16),
                 b_glu[0].reshape(1, D_SSM), w_proj_att[0].astype(BF16), w_proj_ssm[0].astype(BF16),
                 w_out[0].astype(BF16), tm_out)
    return out.reshape(bsz, seq, D_MODEL)
```

```python
import functools
import math

import jax
import jax.numpy as jnp
from jax import lax
from jax.experimental import pallas as pl
from jax.experimental.pallas import tpu as pltpu

F32 = jnp.float32
BF16 = jnp.bfloat16
HIGHEST = lax.Precision.HIGHEST

D_MODEL = 2048
HEADS = 8
DQK = 64
DV = 2 * DQK
D_ATT = HEADS * DV
D_SSM = 1024
GROUP = 16
GROUPS = D_SSM // GROUP
PAIRS = GROUPS // 2
STATE = 64
N_IN = 6 * 1024 + 2 * D_MODEL
RMS_EPS = 1e-6
LAMBDA_INIT = 0.8 - 0.6 * math.exp(-0.3 * 0)
CHUNK = 16
CW = CHUNK * GROUP
SSM_SPLIT = 4
SG = GROUPS // SSM_SPLIT
SP = SG // 2
NEG = -1e30
SCORE_BOUND_FAST = 30.0
KV_GROUP = 4

TILE_Q, TILE_K, TILE_V, TILE_ZATT, TILE_U, TILE_ZSSM = 0, 1, 2, 3, 4, 5
TILE_GATT0, TILE_GSSM0 = 6, 8
N_TILES = N_IN // 1024

VMEM_LIMIT = 56 * 1024 * 1024


def _dot(a, b):
    return jnp.dot(a, b, preferred_element_type=F32)


def _in_proj_kernel(x_ref, ln_ref, w_ref, qg_ref, kg_ref, gsum_ref, proj_ref, vt_ref, h_ref):
    j = pl.program_id(1)

    def group_rms_norm(acc, gain):
        sq = acc * acc
        hi = sq.astype(BF16)
        lo = (sq - hi.astype(F32)).astype(BF16)
        g = gsum_ref[...]
        cols = []
        for c in range(4):
            sl = slice(c * 256, (c + 1) * 256)
            cols.append(_dot(hi[:, sl], g) + _dot(lo[:, sl], g))
        ms = jnp.concatenate(cols, axis=1) * (1.0 / DQK)
        return acc * lax.rsqrt(ms + RMS_EPS) * gain

    @pl.when(j == TILE_Q)
    def _():
        x = x_ref[...]
        ms = jnp.mean(x * x, axis=-1, keepdims=True)
        h = (x * lax.rsqrt(ms + RMS_EPS) * ln_ref[...]).astype(BF16)
        h_ref[...] = h
        proj_ref[...] = group_rms_norm(_dot(h, w_ref[...]), qg_ref[...]).astype(BF16)

    @pl.when(j == TILE_K)
    def _():
        proj_ref[...] = group_rms_norm(_dot(h_ref[...], w_ref[...]), kg_ref[...]).astype(BF16)

    @pl.when(j == TILE_V)
    def _():
        acc = _dot(h_ref[...], w_ref[...])
        proj_ref[...] = acc.astype(BF16)
        vt_ref[...] = acc.T.astype(BF16)

    @pl.when(j == TILE_U)
    def _():
        proj_ref[...] = _dot(h_ref[...], w_ref[...]).astype(BF16)

    @pl.when((j == TILE_ZATT) | (j == TILE_ZSSM))
    def _():
        proj_ref[...] = jax.nn.silu(_dot(h_ref[...], w_ref[...])).astype(BF16)

    @pl.when(j >= TILE_GATT0)
    def _():
        proj_ref[...] = jax.nn.sigmoid(_dot(h_ref[...], w_ref[...])).astype(BF16)


def _in_proj(x2, ln_gain, w_in, q_gain, k_gain, bsz, seq, tm):
    tokens = bsz * seq
    tiles_per_seq = seq // tm
    gi = lax.broadcasted_iota(jnp.int32, (256, 256), 0) // DQK
    gj = lax.broadcasted_iota(jnp.int32, (256, 256), 1) // DQK
    gsum = (gi == gj).astype(BF16)
    return pl.pallas_call(
        _in_proj_kernel,
        grid=(tokens // tm, N_TILES),
        in_specs=[
            pl.BlockSpec((tm, D_MODEL), lambda i, j: (i, 0)),
            pl.BlockSpec((1, D_MODEL), lambda i, j: (0, 0)),
            pl.BlockSpec((D_MODEL, 1024), lambda i, j: (0, j)),
            pl.BlockSpec((1, 1024), lambda i, j: (0, 0)),
            pl.BlockSpec((1, 1024), lambda i, j: (0, 0)),
            pl.BlockSpec((256, 256), lambda i, j: (0, 0)),
        ],
        out_specs=[
            pl.BlockSpec((tm, 1024), lambda i, j: (i, j)),
            pl.BlockSpec((None, D_ATT, tm), lambda i, j: (i // tiles_per_seq, 0, i % tiles_per_seq)),
        ],
        out_shape=[
            jax.ShapeDtypeStruct((tokens, N_IN), BF16),
            jax.ShapeDtypeStruct((bsz, D_ATT, seq), BF16),
        ],
        scratch_shapes=[pltpu.VMEM((tm, D_MODEL), BF16)],
        compiler_params=pltpu.CompilerParams(
            dimension_semantics=("arbitrary", "arbitrary"), vmem_limit_bytes=VMEM_LIMIT),
        name="in_proj",
    )(x2, ln_gain, w_in, q_gain, k_gain, gsum)


def _attn_kernel(lp_ref, sg_ref, q_ref, k_ref, vt_ref, z_ref, o_ref,
                 qbd_ref, m_ref, l_ref, acc_ref, *, seq, tq, fast):
    lp = lp_ref[...]
    lam = (jnp.exp(jnp.sum(lp[0:1] * lp[1:2], axis=-1, keepdims=True))
           - jnp.exp(jnp.sum(lp[2:3] * lp[3:4], axis=-1, keepdims=True)) + LAMBDA_INIT)

    def q_tile(i, carry):
        q0 = pl.multiple_of(i * tq, tq)
        qt = q_ref[pl.ds(q0, tq), :].astype(F32).T
        row = lax.broadcasted_iota(jnp.int32, (DV, tq), 0)
        zero = jnp.zeros_like(qt)
        qbd_ref[:, :tq] = jnp.where(row < DQK, qt, zero).astype(BF16)
        qbd_ref[:, tq:] = jnp.where(row >= DQK, qt, zero).astype(BF16)
        m_ref[...] = jnp.full(m_ref.shape, NEG, F32)
        l_ref[...] = jnp.zeros(l_ref.shape, F32)
        acc_ref[...] = jnp.zeros(acc_ref.shape, F32)

        def block(j, masked, state):
            m, l, acc = state
            k0 = pl.multiple_of(j * tq, tq)
            s = _dot(k_ref[pl.ds(k0, tq), :], qbd_ref[...])
            if masked:
                kpos = lax.broadcasted_iota(jnp.int32, (tq, 2 * tq), 0)
                qpos = lax.broadcasted_iota(jnp.int32, (tq, 2 * tq), 1)
                qpos = jnp.where(qpos >= tq, qpos - tq, qpos)
                s = jnp.where(kpos <= qpos, s, NEG)
            vt = vt_ref[:, pl.ds(k0, tq)]
            if fast:
                p = jnp.exp(s)
                return m, l + jnp.sum(p, axis=0, keepdims=True), acc + _dot(vt, p.astype(BF16))
            m_new = jnp.maximum(m, jnp.max(s, axis=0, keepdims=True))
            alpha = jnp.exp(m - m_new)
            p = jnp.exp(s - m_new)
            return (m_new, alpha * l + jnp.sum(p, axis=0, keepdims=True),
                    alpha * acc + _dot(vt, p.astype(BF16)))

        def load_state():
            return m_ref[...], l_ref[...], acc_ref[...]

        def group(jj, c):
            state = load_state()
            for u in range(KV_GROUP):
                state = block(jj * KV_GROUP + u, False, state)
            m_ref[...], l_ref[...], acc_ref[...] = state
            return c

        n_groups = lax.shift_right_logical(i, KV_GROUP.bit_length() - 1)
        lax.fori_loop(0, n_groups, group, 0)

        for rem in range(KV_GROUP):
            @pl.when((i & (KV_GROUP - 1)) == rem)
            def _():
                state = load_state()
                for u in range(rem):
                    state = block(n_groups * KV_GROUP + u, False, state)
                _, l, acc = block(i, True, state)
                o = acc * (1.0 / l)
                a = o[:, :tq] - lam * o[:, tq:]
                ms = jnp.mean(a * a, axis=0, keepdims=True)
                n = (a * lax.rsqrt(ms + RMS_EPS)).T
                out = n * sg_ref[...] * z_ref[pl.ds(q0, tq), :].astype(F32)
                o_ref[pl.ds(q0, tq), :] = out.astype(BF16)
        return carry

    lax.fori_loop(0, seq // tq, q_tile, 0)


def _attention(lam_params, subln, proj3, vt, *, bsz, seq, tq, fast):
    kern = functools.partial(_attn_kernel, seq=seq, tq=tq, fast=fast)
    qcol, kcol, zcol = TILE_Q * HEADS, TILE_K * HEADS, TILE_ZATT * HEADS
    return pl.pallas_call(
        kern,
        grid=(bsz, HEADS),
        in_specs=[
            pl.BlockSpec((4, DQK), lambda b, h: (0, 0)),
            pl.BlockSpec((1, DV), lambda b, h: (0, 0)),
            pl.BlockSpec((None, seq, DV), lambda b, h: (b, 0, qcol + h)),
            pl.BlockSpec((None, seq, DV), lambda b, h: (b, 0, kcol + h)),
            pl.BlockSpec((None, DV, seq), lambda b, h: (b, h, 0)),
            pl.BlockSpec((None, seq, DV), lambda b, h: (b, 0, zcol + h)),
        ],
        out_specs=pl.BlockSpec((None, seq, DV), lambda b, h: (b, 0, h)),
        out_shape=jax.ShapeDtypeStruct((bsz, seq, D_ATT), BF16),
        scratch_shapes=[
            pltpu.VMEM((DV, 2 * tq), BF16),
            pltpu.VMEM((1, 2 * tq), F32),
            pltpu.VMEM((1, 2 * tq), F32),
            pltpu.VMEM((DV, 2 * tq), F32),
        ],
        compiler_params=pltpu.CompilerParams(
            dimension_semantics=("arbitrary", "arbitrary"), vmem_limit_bytes=VMEM_LIMIT),
        name="diff_attention",
    )(lam_params, subln, proj3, proj3, vt, proj3)


def _ssm_param_kernel(ldt_ref, lre_ref, lim_ref, lrec_ref, limc_ref, btre_ref, btim_ref,
                      c4re_ref, c4im_ref, bpre_ref, bpim_ref, cre_ref, cim_ref,
                      tt_ref, w_ref, vt_ref):
    h = pl.program_id(0) % 2
    dt = jnp.exp(ldt_ref[...])

    def cis_pow(n, lr_, li_):
        mag = jnp.exp(n * (lr_ * dt))
        ang = n * (li_ * dt)
        return mag * jnp.cos(ang), mag * jnp.sin(ang)

    def zoh(lr_, li_):
        a_re, a_im = cis_pow(1.0, lr_, li_)
        nr, ni = a_re - 1.0, a_im
        den = lr_ * lr_ + li_ * li_
        return (nr * lr_ + ni * li_) / den, (ni * lr_ - nr * li_) / den

    def complex_mul(x_re, x_im, y_re, y_im):
        return x_re * y_re - x_im * y_im, x_re * y_im + x_im * y_re

    lr, li = lre_ref[...], lim_ref[...]
    coef_re, coef_im = zoh(lr, li)
    bb_re, bb_im = complex_mul(coef_re, coef_im, btre_ref[...], btim_ref[...])
    n_rows = jnp.minimum(lax.broadcasted_iota(jnp.int32, (24, CW), 0), CHUNK).astype(F32)
    p_re, p_im = cis_pow(n_rows, lr, li)
    c4_re, c4_im = c4re_ref[...], c4im_ref[...]
    slot = lax.broadcasted_iota(jnp.int32, (GROUP, CW), 1) // STATE

    def to_slots(v_re, v_im):
        return jnp.where(slot == h, v_re, jnp.where(slot == h + 2, v_im, 0.0)).astype(BF16)

    for s in range(CHUNK):
        n = CHUNK - 1 - s
        w_re, w_im = complex_mul(bb_re, bb_im, p_re[n:n + 1], p_im[n:n + 1])
        w_ref[s * GROUP:(s + 1) * GROUP, :] = to_slots(w_re, w_im)
        v_re, v_im = complex_mul(c4_re, c4_im, p_re[s + 1:s + 2], p_im[s + 1:s + 2])
        vt_ref[s * GROUP:(s + 1) * GROUP, :] = to_slots(v_re, -v_im)

    lrc, lic = lrec_ref[...], limc_ref[...]
    cc_re, cc_im = zoh(lrc, lic)
    bp_re, bp_im = complex_mul(cc_re, cc_im, bpre_ref[...], bpim_ref[...])
    n_lanes = jnp.minimum(lax.broadcasted_iota(jnp.int32, (STATE, 128), 1), CHUNK).astype(F32)
    q_re, q_im = cis_pow(n_lanes, lrc, lic)
    sel = lax.broadcasted_iota(jnp.int32, (128, CW), 0)
    lane_r = lax.broadcasted_iota(jnp.int32, (128, CW), 1) // GROUP
    lane_ci = lax.broadcasted_iota(jnp.int32, (128, CW), 1) % GROUP
    e_pow = (sel == CHUNK - 1 - lane_r).astype(F32)
    e_ci = (sel == lane_ci).astype(F32)

    def hdot(a, b):
        return jnp.dot(a, b, preferred_element_type=F32, precision=HIGHEST)

    r_re, r_im = complex_mul(hdot(q_re, e_pow), hdot(q_im, e_pow), hdot(bp_re, e_ci), hdot(bp_im, e_ci))
    krev = hdot(cre_ref[...], r_re) - hdot(cim_ref[...], r_im)
    lane_s = lax.broadcasted_iota(jnp.int32, (GROUP, CW), 1) // GROUP
    for t in range(CHUNK):
        shift = (CHUNK - 1 - t) * GROUP
        moved = krev if shift == 0 else pltpu.roll(krev, CW - shift, axis=1)
        tt_ref[t * GROUP:(t + 1) * GROUP, :] = jnp.where(lane_s <= t, moved, 0.0).astype(BF16)


def _ssm_params(log_dt, lam_re, lam_im, b_re, b_im, c_re, c_im):
    tile4 = lambda a: jnp.tile(a, (1, 1, 4))
    pad128 = lambda a: jnp.pad(a, ((0, 0), (0, 0), (0, 128 - GROUP)))
    ldt = log_dt.reshape(GROUPS, 1, 1)
    lre4 = tile4(lam_re.reshape(GROUPS, 1, STATE))
    lim4 = tile4(lam_im.reshape(GROUPS, 1, STATE))
    lrec = lam_re.reshape(GROUPS, STATE, 1)
    limc = lam_im.reshape(GROUPS, STATE, 1)
    bt_re = tile4(jnp.swapaxes(b_re, 1, 2))
    bt_im = tile4(jnp.swapaxes(b_im, 1, 2))
    c4_re, c4_im = tile4(c_re), tile4(c_im)
    bp_re, bp_im = pad128(b_re), pad128(b_im)

    def spec(shape):
        return pl.BlockSpec((None,) + shape, lambda g: (g, 0, 0))

    return pl.pallas_call(
        _ssm_param_kernel,
        grid=(GROUPS,),
        in_specs=[spec((1, 1)), spec((1, CW)), spec((1, CW)), spec((STATE, 1)), spec((STATE, 1)),
                  spec((GROUP, CW)), spec((GROUP, CW)), spec((GROUP, CW)), spec((GROUP, CW)),
                  spec((STATE, 128)), spec((STATE, 128)), spec((GROUP, STATE)), spec((GROUP, STATE))],
        out_specs=[
            pl.BlockSpec((None, CW, CW), lambda g: (g, 0, 0)),
            pl.BlockSpec((None, CW, CW), lambda g: (g // 2, g % 2, 0)),
            pl.BlockSpec((None, CW, CW), lambda g: (g, 0, 0)),
        ],
        out_shape=[
            jax.ShapeDtypeStruct((GROUPS, CW, CW), BF16),
            jax.ShapeDtypeStruct((PAIRS, 2 * CW, CW), BF16),
            jax.ShapeDtypeStruct((GROUPS, CW, CW), BF16),
        ],
        compiler_params=pltpu.CompilerParams(dimension_semantics=("arbitrary",)),
        name="ssm_params",
    )(ldt, lre4, lim4, lrec, limc, bt_re, bt_im, c4_re, c4_im, bp_re, bp_im, c_re, c_im)


def _ssm_kernel(u_ref, tt_ref, w_ref, vt_ref, d_ref, are_ref, aim_ref, y_ref,
                uf_ref, lt_ref, sre_ref, sim_ref, yt_ref, st_ref, xr_ref, xi_ref, *, bsz, ct):
    cols = bsz * ct
    halves = [slice(0, 128), slice(128, 256)]

    @pl.when(pl.program_id(1) == 0)
    def _():
        xr_ref[...] = jnp.zeros(xr_ref.shape, F32)
        xi_ref[...] = jnp.zeros(xi_ref.shape, F32)

    for b in range(bsz):
        u = u_ref[b].astype(F32)
        for hf, sl in enumerate(halves):
            uf_ref[b, hf] = u[:, sl]
    for s in range(CHUNK):
        x_s = jnp.concatenate(
            [jnp.concatenate([uf_ref[b, hf, pl.ds(s, ct, stride=CHUNK), :] for b in range(bsz)], axis=0)
             for hf in range(2)], axis=1)
        lt_ref[:, s * GROUP:(s + 1) * GROUP, :] = x_s.T.astype(BF16).reshape(SG, GROUP, cols)

    def pair_rows(gp):
        return slice(gp * cols, (gp + 1) * cols)

    for gp in range(SP):
        lt_pair = lt_ref[2 * gp:2 * gp + 2].reshape(2 * CW, cols)
        s = lax.dot_general(lt_pair, w_ref[gp], (((0,), (0,)), ((), ())), preferred_element_type=F32)
        sre_ref[pair_rows(gp), :] = s[:, :128]
        sim_ref[pair_rows(gp), :] = s[:, 128:]

    a_re, a_im = are_ref[...], aim_ref[...]

    def scan_step(n, c):
        rows = pl.ds(n, SP * bsz, stride=ct)
        s_re, s_im = sre_ref[rows, :], sim_ref[rows, :]
        x_re, x_im = xr_ref[...], xi_ref[...]
        sre_ref[rows, :] = x_re
        sim_ref[rows, :] = x_im
        xr_ref[...] = a_re * x_re - a_im * x_im + s_re
        xi_ref[...] = a_re * x_im + a_im * x_re + s_im
        return c

    lax.fori_loop(0, ct, scan_step, 0, unroll=4)

    for gp in range(SP):
        x_in = jnp.concatenate([sre_ref[pair_rows(gp), :], sim_ref[pair_rows(gp), :]],
                               axis=1).astype(BF16)
        for hh in range(2):
            g = 2 * gp + hh
            yt_ref[g] = _dot(tt_ref[g], lt_ref[g]) + lax.dot_general(
                vt_ref[g], x_in, (((1,), (1,)), ((), ())), preferred_element_type=F32)

    for t in range(CHUNK):
        z = yt_ref[:, t * GROUP:(t + 1) * GROUP, :].reshape(SG * GROUP, cols).T
        for b in range(bsz):
            for hf, sl in enumerate(halves):
                st_ref[b, hf, pl.ds(t, ct, stride=CHUNK), :] = z[b * ct:(b + 1) * ct, sl]
    for b in range(bsz):
        pre = jnp.concatenate([st_ref[b, 0], st_ref[b, 1]], axis=1)
        u = jnp.concatenate([uf_ref[b, 0], uf_ref[b, 1]], axis=1)
        y_ref[b] = jax.nn.gelu(pre + d_ref[...] * u).astype(BF16)


def _ssm(proj3, tt_mat, w_mat, vt_mat, d_skip, a_re, a_im, bsz, seq, ct):
    tok = ct * CHUNK
    cols = bsz * ct
    kern = functools.partial(_ssm_kernel, bsz=bsz, ct=ct)
    ucol = TILE_U * SSM_SPLIT
    return pl.pallas_call(
        kern,
        grid=(SSM_SPLIT, seq // tok),
        in_specs=[
            pl.BlockSpec((bsz, tok, CW), lambda q, t: (0, t, ucol + q)),
            pl.BlockSpec((SG, CW, CW), lambda q, t: (q, 0, 0)),
            pl.BlockSpec((SP, 2 * CW, CW), lambda q, t: (q, 0, 0)),
            pl.BlockSpec((SG, CW, CW), lambda q, t: (q, 0, 0)),
            pl.BlockSpec((1, CW), lambda q, t: (0, q)),
            pl.BlockSpec((None, SP * bsz, 128), lambda q, t: (q, 0, 0)),
            pl.BlockSpec((None, SP * bsz, 128), lambda q, t: (q, 0, 0)),
        ],
        out_specs=pl.BlockSpec((bsz, tok, CW), lambda q, t: (0, t, q)),
        out_shape=jax.ShapeDtypeStruct((bsz, seq, D_SSM), BF16),
        scratch_shapes=[
            pltpu.VMEM((bsz, 2, tok, 128), F32),
            pltpu.VMEM((SG, CW, cols), BF16),
            pltpu.VMEM((SP * cols, 128), F32),
            pltpu.VMEM((SP * cols, 128), F32),
            pltpu.VMEM((SG, CW, cols), F32),
            pltpu.VMEM((bsz, 2, tok, 128), F32),
            pltpu.VMEM((SP * bsz, 128), F32),
            pltpu.VMEM((SP * bsz, 128), F32),
        ],
        compiler_params=pltpu.CompilerParams(
            dimension_semantics=("arbitrary", "arbitrary"), vmem_limit_bytes=VMEM_LIMIT),
        name="ssm_scan",
    )(proj3, tt_mat, w_mat, vt_mat, d_skip, a_re, a_im)


def _chunk_decay_kernel(ldt_ref, lre_ref, lim_ref, are_ref, aim_ref):
    dt = jnp.exp(ldt_ref[...])
    mag = jnp.exp(float(CHUNK) * (lre_ref[...] * dt))
    ang = float(CHUNK) * (lim_ref[...] * dt)
    are_ref[...] = mag * jnp.cos(ang)
    aim_ref[...] = mag * jnp.sin(ang)


def _chunk_decay(ldt_p, lre_p, lim_p):
    shape = jax.ShapeDtypeStruct(ldt_p.shape, F32)
    return pl.pallas_call(_chunk_decay_kernel, out_shape=[shape, shape], name="ssm_chunk_decay")(
        ldt_p, lre_p, lim_p)


def _merge_kernel(x_ref, ya_ref, yg_ref, zs_ref, ga_ref, gs_ref, wg_ref, bg_ref,
                  wpa_ref, wps_ref, wo_ref, o_ref):
    yg = yg_ref[...]
    glu = _dot(yg, wg_ref[...]) + bg_ref[...]
    y_ssm = (yg.astype(F32) * jax.nn.sigmoid(glu) * zs_ref[...].astype(F32)).astype(BF16)
    merged = (ga_ref[...].astype(F32) * _dot(ya_ref[...], wpa_ref[...])
              + gs_ref[...].astype(F32) * _dot(y_ssm, wps_ref[...]))
    o_ref[...] = x_ref[...] + _dot(merged.astype(BF16), wo_ref[...])


def _merge(x2, y_att, yg, proj, w_glu, b_glu, w_pa, w_ps, w_out, tm):
    tokens = x2.shape[0]
    const = lambda i: (0, 0)
    once = pl.Buffered(1)
    return pl.pallas_call(
        _merge_kernel,
        grid=(tokens // tm,),
        in_specs=[
            pl.BlockSpec((tm, D_MODEL), lambda i: (i, 0)),
            pl.BlockSpec((tm, D_ATT), lambda i: (i, 0)),
            pl.BlockSpec((tm, D_SSM), lambda i: (i, 0)),
            pl.BlockSpec((tm, D_SSM), lambda i: (i, TILE_ZSSM)),
            pl.BlockSpec((tm, D_MODEL), lambda i: (i, TILE_GATT0 // 2)),
            pl.BlockSpec((tm, D_MODEL), lambda i: (i, TILE_GSSM0 // 2)),
            pl.BlockSpec((D_SSM, D_SSM), const, pipeline_mode=once),
            pl.BlockSpec((1, D_SSM), const, pipeline_mode=once),
            pl.BlockSpec((D_ATT, D_MODEL), const, pipeline_mode=once),
            pl.BlockSpec((D_SSM, D_MODEL), const, pipeline_mode=once),
            pl.BlockSpec((D_MODEL, D_MODEL), const, pipeline_mode=once),
        ],
        out_specs=pl.BlockSpec((tm, D_MODEL), lambda i: (i, 0)),
        out_shape=jax.ShapeDtypeStruct((tokens, D_MODEL), F32),
        compiler_params=pltpu.CompilerParams(
            dimension_semantics=("arbitrary",), vmem_limit_bytes=VMEM_LIMIT),
        name="merge_out",
    )(x2, y_att, yg, proj, proj, proj, w_glu, b_glu, w_pa, w_ps, w_out)


def kernel(x, ln_gain, w_in, q_norm_gain, k_norm_gain, lambda_q1, lambda_k1, lambda_q2, lambda_k2,
           subln_gain, ssm_lambda_re, ssm_lambda_im, ssm_log_dt, ssm_b_re, ssm_b_im, ssm_c_re,
           ssm_c_im, ssm_d, w_glu, b_glu, w_proj_att, w_proj_ssm, w_out):
    bsz, seq, _ = x.shape
    assert ln_gain.shape[0] == 1 and x.shape[2] == D_MODEL and w_in.shape[2] == N_IN
    tokens = bsz * seq
    tm_in = min(1024, seq)
    tq = min(512, seq)
    tm_out = min(512, seq)
    ct = 128 // bsz
    assert 128 % bsz == 0 and ct % 8 == 0 and seq % (ct * CHUNK) == 0

    x2 = x.reshape(tokens, D_MODEL)
    scale = DQK ** -0.5
    q_gain = (jnp.tile(q_norm_gain[0], 2 * HEADS) * scale).reshape(1, 1024)
    k_gain = jnp.tile(k_norm_gain[0], 2 * HEADS).reshape(1, 1024)
    proj, vt = _in_proj(x2, ln_gain[0].reshape(1, D_MODEL), w_in[0].astype(BF16), q_gain, k_gain,
                        bsz, seq, tm_in)
    proj3 = proj.reshape(bsz, seq, N_IN)

    lam_params = jnp.stack([lambda_q1[0], lambda_k1[0], lambda_q2[0], lambda_k2[0]])
    subln = (subln_gain[0] * (1.0 - LAMBDA_INIT)).reshape(1, DV)
    score_bound = 1.02 * 8.0 * jnp.max(jnp.abs(q_norm_gain[0] * k_norm_gain[0]))
    attend = functools.partial(_attention, bsz=bsz, seq=seq, tq=tq)
    y_att = lax.cond(score_bound <= SCORE_BOUND_FAST,
                     functools.partial(attend, fast=True), functools.partial(attend, fast=False),
                     lam_params, subln, proj3, vt)

    tt_mat, w_mat, vt_mat = _ssm_params(ssm_log_dt[0], ssm_lambda_re[0], ssm_lambda_im[0],
                                        ssm_b_re[0], ssm_b_im[0], ssm_c_re[0], ssm_c_im[0])
    per_row = lambda a: jnp.repeat(a.reshape(PAIRS, 128), bsz, axis=0).reshape(SSM_SPLIT, SP * bsz, 128)
    a_re, a_im = _chunk_decay(per_row(jnp.repeat(ssm_log_dt[0], STATE)), per_row(ssm_lambda_re[0]),
                              per_row(ssm_lambda_im[0]))
    yg = _ssm(proj3, tt_mat, w_mat, vt_mat, ssm_d[0].reshape(1, D_SSM), a_re, a_im, bsz, seq, ct)

    out = _merge(x2, y_att.reshape(tokens, D_ATT), yg.reshape(tokens, D_SSM), proj, w_glu[0].astype(BF16),
                 b_glu[0].reshape(1, D_SSM), w_proj_att[0].astype(BF16), w_proj_ssm[0].astype(BF16),
                 w_out[0].astype(BF16), tm_out)
    return out.reshape(bsz, seq, D_MODEL)
```

```python
import functools
import math

import jax
import jax.numpy as jnp
from jax import lax
from jax.experimental import pallas as pl
from jax.experimental.pallas import tpu as pltpu

F32 = jnp.float32
BF16 = jnp.bfloat16
HIGHEST = lax.Precision.HIGHEST

D_MODEL = 2048
HEADS = 8
DQK = 64
DV = 2 * DQK
D_ATT = HEADS * DV
D_SSM = 1024
GROUP = 16
GROUPS = D_SSM // GROUP
PAIRS = GROUPS // 2
STATE = 64
N_IN = 6 * 1024 + 2 * D_MODEL
RMS_EPS = 1e-6
LAMBDA_INIT = 0.8 - 0.6 * math.exp(-0.3 * 0)
CHUNK = 16
CW = CHUNK * GROUP
SSM_SPLIT = 4
PARAM_BATCH = 4
SG = GROUPS // SSM_SPLIT
SP = SG // 2
NEG = -1e30
SCORE_BOUND_FAST = 30.0
KV_GROUP = 2

TILE_Q, TILE_K, TILE_V, TILE_ZATT, TILE_U, TILE_ZSSM = 0, 1, 2, 3, 4, 5
TILE_GATT0, TILE_GSSM0 = 6, 8
N_TILES = N_IN // 1024

VMEM_LIMIT = 56 * 1024 * 1024


def _dot(a, b):
    return jnp.dot(a, b, preferred_element_type=F32)


def _sigmoid(x):
    return 0.5 * jnp.tanh(0.5 * x) + 0.5


def _in_proj_kernel(x_ref, ln_ref, w_ref, qg_ref, kg_ref, gsum_ref, proj_ref, vt_ref, h_ref):
    j = pl.program_id(1)

    def group_rms_norm(acc, gain):
        sq = (acc * acc).astype(BF16)
        g = gsum_ref[...]
        ms = jnp.concatenate([_dot(sq[:, c * 256:(c + 1) * 256], g) for c in range(4)],
                             axis=1) * (1.0 / DQK)
        return acc * lax.rsqrt(ms + RMS_EPS) * gain

    @pl.when(j == TILE_Q)
    def _():
        x = x_ref[...]
        ms = jnp.mean(x * x, axis=-1, keepdims=True)
        h = (x * lax.rsqrt(ms + RMS_EPS) * ln_ref[...]).astype(BF16)
        h_ref[...] = h
        proj_ref[...] = group_rms_norm(_dot(h, w_ref[...]), qg_ref[...]).astype(BF16)

    @pl.when(j == TILE_K)
    def _():
        proj_ref[...] = group_rms_norm(_dot(h_ref[...], w_ref[...]), kg_ref[...]).astype(BF16)

    @pl.when(j == TILE_V)
    def _():
        acc = _dot(h_ref[...], w_ref[...])
        proj_ref[...] = acc.astype(BF16)
        vt_ref[...] = acc.T.astype(BF16)

    @pl.when(j == TILE_U)
    def _():
        proj_ref[...] = _dot(h_ref[...], w_ref[...]).astype(BF16)

    @pl.when((j == TILE_ZATT) | (j == TILE_ZSSM))
    def _():
        acc = _dot(h_ref[...], w_ref[...])
        proj_ref[...] = (acc * _sigmoid(acc)).astype(BF16)

    @pl.when(j >= TILE_GATT0)
    def _():
        proj_ref[...] = _sigmoid(_dot(h_ref[...], w_ref[...])).astype(BF16)


def _in_proj(x2, ln_gain, w_in, q_gain, k_gain, bsz, seq, tm):
    tokens = bsz * seq
    tiles_per_seq = seq // tm
    gi = lax.broadcasted_iota(jnp.int32, (256, 256), 0) // DQK
    gj = lax.broadcasted_iota(jnp.int32, (256, 256), 1) // DQK
    gsum = (gi == gj).astype(BF16)
    return pl.pallas_call(
        _in_proj_kernel,
        grid=(tokens // tm, N_TILES),
        in_specs=[
            pl.BlockSpec((tm, D_MODEL), lambda i, j: (i, 0)),
            pl.BlockSpec((1, D_MODEL), lambda i, j: (0, 0)),
            pl.BlockSpec((D_MODEL, 1024), lambda i, j: (0, j)),
            pl.BlockSpec((1, 1024), lambda i, j: (0, 0)),
            pl.BlockSpec((1, 1024), lambda i, j: (0, 0)),
            pl.BlockSpec((256, 256), lambda i, j: (0, 0)),
        ],
        out_specs=[
            pl.BlockSpec((tm, 1024), lambda i, j: (i, j)),
            pl.BlockSpec((None, D_ATT, tm), lambda i, j: (i // tiles_per_seq, 0, i % tiles_per_seq)),
        ],
        out_shape=[
            jax.ShapeDtypeStruct((tokens, N_IN), BF16),
            jax.ShapeDtypeStruct((bsz, D_ATT, seq), BF16),
        ],
        scratch_shapes=[pltpu.VMEM((tm, D_MODEL), BF16)],
        compiler_params=pltpu.CompilerParams(
            dimension_semantics=("arbitrary", "arbitrary"), vmem_limit_bytes=VMEM_LIMIT),
        name="in_proj",
    )(x2, ln_gain, w_in, q_gain, k_gain, gsum)


def _attn_kernel(lp_ref, sg_ref, q_ref, k_ref, vt_ref, z_ref, o_ref,
                 qbd_ref, m_ref, l_ref, acc_ref, *, seq, tq, fast):
    lp = lp_ref[...]
    lam = (jnp.exp(jnp.sum(lp[0:1] * lp[1:2], axis=-1, keepdims=True))
           - jnp.exp(jnp.sum(lp[2:3] * lp[3:4], axis=-1, keepdims=True)) + LAMBDA_INIT)

    def q_tile(i, carry):
        q0 = pl.multiple_of(i * tq, tq)
        qt = q_ref[pl.ds(q0, tq), :].astype(F32).T
        row = lax.broadcasted_iota(jnp.int32, (DV, tq), 0)
        zero = jnp.zeros_like(qt)
        qbd_ref[:, :tq] = jnp.where(row < DQK, qt, zero).astype(BF16)
        qbd_ref[:, tq:] = jnp.where(row >= DQK, qt, zero).astype(BF16)
        m_ref[...] = jnp.full(m_ref.shape, NEG, F32)
        l_ref[...] = jnp.zeros(l_ref.shape, F32)
        acc_ref[...] = jnp.zeros(acc_ref.shape, F32)

        def block(j, masked, state):
            m, l, acc = state
            k0 = pl.multiple_of(j * tq, tq)
            s = _dot(k_ref[pl.ds(k0, tq), :], qbd_ref[...])
            if masked:
                kpos = lax.broadcasted_iota(jnp.int32, (tq, 2 * tq), 0)
                qpos = lax.broadcasted_iota(jnp.int32, (tq, 2 * tq), 1)
                qpos = jnp.where(qpos >= tq, qpos - tq, qpos)
                s = jnp.where(kpos <= qpos, s, NEG)
            vt = vt_ref[:, pl.ds(k0, tq)]
            if fast:
                p = jnp.exp(s)
                return m, l + jnp.sum(p, axis=0, keepdims=True), acc + _dot(vt, p.astype(BF16))
            m_new = jnp.maximum(m, jnp.max(s, axis=0, keepdims=True))
            alpha = jnp.exp(m - m_new)
            p = jnp.exp(s - m_new)
            return (m_new, alpha * l + jnp.sum(p, axis=0, keepdims=True),
                    alpha * acc + _dot(vt, p.astype(BF16)))

        def load_state():
            return m_ref[...], l_ref[...], acc_ref[...]

        def group(jj, c):
            state = load_state()
            for u in range(KV_GROUP):
                state = block(jj * KV_GROUP + u, False, state)
            m_ref[...], l_ref[...], acc_ref[...] = state
            return c

        n_groups = lax.shift_right_logical(i, KV_GROUP.bit_length() - 1)
        lax.fori_loop(0, n_groups, group, 0)

        for rem in range(KV_GROUP):
            @pl.when((i & (KV_GROUP - 1)) == rem)
            def _():
                state = load_state()
                for u in range(rem):
                    state = block(n_groups * KV_GROUP + u, False, state)
                _, l, acc = block(i, True, state)
                o = acc * (1.0 / l)
                a = o[:, :tq] - lam * o[:, tq:]
                ms = jnp.mean(a * a, axis=0, keepdims=True)
                n = (a * lax.rsqrt(ms + RMS_EPS)).T
                out = n * sg_ref[...] * z_ref[pl.ds(q0, tq), :].astype(F32)
                o_ref[pl.ds(q0, tq), :] = out.astype(BF16)
        return carry

    lax.fori_loop(0, seq // tq, q_tile, 0)


def _attention(lam_params, subln, proj3, vt, *, bsz, seq, tq, fast):
    kern = functools.partial(_attn_kernel, seq=seq, tq=tq, fast=fast)
    qcol, kcol, zcol = TILE_Q * HEADS, TILE_K * HEADS, TILE_ZATT * HEADS
    return pl.pallas_call(
        kern,
        grid=(bsz, HEADS),
        in_specs=[
            pl.BlockSpec((4, DQK), lambda b, h: (0, 0)),
            pl.BlockSpec((1, DV), lambda b, h: (0, 0)),
            pl.BlockSpec((None, seq, DV), lambda b, h: (b, 0, qcol + h)),
            pl.BlockSpec((None, seq, DV), lambda b, h: (b, 0, kcol + h)),
            pl.BlockSpec((None, DV, seq), lambda b, h: (b, h, 0)),
            pl.BlockSpec((None, seq, DV), lambda b, h: (b, 0, zcol + h)),
        ],
        out_specs=pl.BlockSpec((None, seq, DV), lambda b, h: (b, 0, h)),
        out_shape=jax.ShapeDtypeStruct((bsz, seq, D_ATT), BF16),
        scratch_shapes=[
            pltpu.VMEM((DV, 2 * tq), BF16),
            pltpu.VMEM((1, 2 * tq), F32),
            pltpu.VMEM((1, 2 * tq), F32),
            pltpu.VMEM((DV, 2 * tq), F32),
        ],
        compiler_params=pltpu.CompilerParams(
            dimension_semantics=("arbitrary", "arbitrary"), vmem_limit_bytes=VMEM_LIMIT),
        name="diff_attention",
    )(lam_params, subln, proj3, proj3, vt, proj3)


def _ssm_param_kernel(ldt_ref, lre_ref, lim_ref, lrec_ref, limc_ref, btre_ref, btim_ref,
                      c4re_ref, c4im_ref, bpre_ref, bpim_ref, cre_ref, cim_ref,
                      tt_ref, w_ref, vt_ref):
    for gi in range(PARAM_BATCH):
        _ssm_param_group(
            gi % 2, ldt_ref.at[gi], lre_ref.at[gi], lim_ref.at[gi], lrec_ref.at[gi], limc_ref.at[gi],
            btre_ref.at[gi], btim_ref.at[gi], c4re_ref.at[gi], c4im_ref.at[gi], bpre_ref.at[gi],
            bpim_ref.at[gi], cre_ref.at[gi], cim_ref.at[gi],
            tt_ref.at[gi], w_ref.at[gi // 2, pl.ds((gi % 2) * CW, CW)], vt_ref.at[gi])


def _ssm_param_group(h, ldt_ref, lre_ref, lim_ref, lrec_ref, limc_ref, btre_ref, btim_ref,
                     c4re_ref, c4im_ref, bpre_ref, bpim_ref, cre_ref, cim_ref, tt_ref, w_ref, vt_ref):
    dt = jnp.exp(ldt_ref[...])

    def cis_pow(n, lr_, li_):
        mag = jnp.exp(n * (lr_ * dt))
        ang = n * (li_ * dt)
        return mag * jnp.cos(ang), mag * jnp.sin(ang)

    def zoh(lr_, li_):
        a_re, a_im = cis_pow(1.0, lr_, li_)
        nr, ni = a_re - 1.0, a_im
        den = lr_ * lr_ + li_ * li_
        return (nr * lr_ + ni * li_) / den, (ni * lr_ - nr * li_) / den

    def complex_mul(x_re, x_im, y_re, y_im):
        return x_re * y_re - x_im * y_im, x_re * y_im + x_im * y_re

    lr, li = lre_ref[...], lim_ref[...]
    coef_re, coef_im = zoh(lr, li)
    bb_re, bb_im = complex_mul(coef_re, coef_im, btre_ref[...], btim_ref[...])
    n_rows = jnp.minimum(lax.broadcasted_iota(jnp.int32, (24, CW), 0), CHUNK).astype(F32)
    p_re, p_im = cis_pow(n_rows, lr, li)
    c4_re, c4_im = c4re_ref[...], c4im_ref[...]
    slot = lax.broadcasted_iota(jnp.int32, (GROUP, CW), 1) // STATE

    def to_slots(v_re, v_im):
        return jnp.where(slot == h, v_re, jnp.where(slot == h + 2, v_im, 0.0)).astype(BF16)

    for s in range(CHUNK):
        n = CHUNK - 1 - s
        w_re, w_im = complex_mul(bb_re, bb_im, p_re[n:n + 1], p_im[n:n + 1])
        w_ref[s * GROUP:(s + 1) * GROUP, :] = to_slots(w_re, w_im)
        v_re, v_im = complex_mul(c4_re, c4_im, p_re[s + 1:s + 2], p_im[s + 1:s + 2])
        vt_ref[s * GROUP:(s + 1) * GROUP, :] = to_slots(v_re, -v_im)

    lrc, lic = lrec_ref[...], limc_ref[...]
    cc_re, cc_im = zoh(lrc, lic)
    bp_re, bp_im = complex_mul(cc_re, cc_im, bpre_ref[...], bpim_ref[...])
    n_lanes = jnp.minimum(lax.broadcasted_iota(jnp.int32, (STATE, 128), 1), CHUNK).astype(F32)
    q_re, q_im = cis_pow(n_lanes, lrc, lic)
    sel = lax.broadcasted_iota(jnp.int32, (128, CW), 0)
    lane_r = lax.broadcasted_iota(jnp.int32, (128, CW), 1) // GROUP
    lane_ci = lax.broadcasted_iota(jnp.int32, (128, CW), 1) % GROUP
    e_pow = (sel == CHUNK - 1 - lane_r).astype(F32)
    e_ci = (sel == lane_ci).astype(F32)

    def hdot(a, b):
        return jnp.dot(a, b, preferred_element_type=F32, precision=HIGHEST)

    r_re, r_im = complex_mul(hdot(q_re, e_pow), hdot(q_im, e_pow), hdot(bp_re, e_ci), hdot(bp_im, e_ci))
    krev = hdot(cre_ref[...], r_re) - hdot(cim_ref[...], r_im)
    lane_s = lax.broadcasted_iota(jnp.int32, (GROUP, CW), 1) // GROUP
    for t in range(CHUNK):
        shift = (CHUNK - 1 - t) * GROUP
        moved = krev if shift == 0 else pltpu.roll(krev, CW - shift, axis=1)
        tt_ref[t * GROUP:(t + 1) * GROUP, :] = jnp.where(lane_s <= t, moved, 0.0).astype(BF16)


def _ssm_params(log_dt, lam_re, lam_im, b_re, b_im, c_re, c_im):
    tile4 = lambda a: jnp.tile(a, (1, 1, 4))
    pad128 = lambda a: jnp.pad(a, ((0, 0), (0, 0), (0, 128 - GROUP)))
    ldt = log_dt.reshape(GROUPS, 1, 1)
    lre4 = tile4(lam_re.reshape(GROUPS, 1, STATE))
    lim4 = tile4(lam_im.reshape(GROUPS, 1, STATE))
    lrec = lam_re.reshape(GROUPS, STATE, 1)
    limc = lam_im.reshape(GROUPS, STATE, 1)
    bt_re = tile4(jnp.swapaxes(b_re, 1, 2))
    bt_im = tile4(jnp.swapaxes(b_im, 1, 2))
    c4_re, c4_im = tile4(c_re), tile4(c_im)
    bp_re, bp_im = pad128(b_re), pad128(b_im)

    def spec(shape):
        return pl.BlockSpec((PARAM_BATCH,) + shape, lambda g: (g, 0, 0))

    return pl.pallas_call(
        _ssm_param_kernel,
        grid=(GROUPS // PARAM_BATCH,),
        in_specs=[spec((1, 1)), spec((1, CW)), spec((1, CW)), spec((STATE, 1)), spec((STATE, 1)),
                  spec((GROUP, CW)), spec((GROUP, CW)), spec((GROUP, CW)), spec((GROUP, CW)),
                  spec((STATE, 128)), spec((STATE, 128)), spec((GROUP, STATE)), spec((GROUP, STATE))],
        out_specs=[
            pl.BlockSpec((PARAM_BATCH, CW, CW), lambda g: (g, 0, 0)),
            pl.BlockSpec((PARAM_BATCH // 2, 2 * CW, CW), lambda g: (g, 0, 0)),
            pl.BlockSpec((PARAM_BATCH, CW, CW), lambda g: (g, 0, 0)),
        ],
        out_shape=[
            jax.ShapeDtypeStruct((GROUPS, CW, CW), BF16),
            jax.ShapeDtypeStruct((PAIRS, 2 * CW, CW), BF16),
            jax.ShapeDtypeStruct((GROUPS, CW, CW), BF16),
        ],
        compiler_params=pltpu.CompilerParams(dimension_semantics=("arbitrary",)),
        name="ssm_params",
    )(ldt, lre4, lim4, lrec, limc, bt_re, bt_im, c4_re, c4_im, bp_re, bp_im, c_re, c_im)


def _ssm_kernel(u_ref, tt_ref, w_ref, vt_ref, d_ref, are_ref, aim_ref, y_ref,
                uf_ref, lt_ref, sre_ref, sim_ref, yt_ref, st_ref, xr_ref, xi_ref, *, bsz, ct):
    cols = bsz * ct
    halves = [slice(0, 128), slice(128, 256)]

    @pl.when(pl.program_id(1) == 0)
    def _():
        xr_ref[...] = jnp.zeros(xr_ref.shape, F32)
        xi_ref[...] = jnp.zeros(xi_ref.shape, F32)

    for b in range(bsz):
        u = u_ref[b].astype(F32)
        for hf, sl in enumerate(halves):
            uf_ref[b, hf] = u[:, sl]
    for s in range(CHUNK):
        x_s = jnp.concatenate(
            [jnp.concatenate([uf_ref[b, hf, pl.ds(s, ct, stride=CHUNK), :] for b in range(bsz)], axis=0)
             for hf in range(2)], axis=1)
        lt_ref[:, s * GROUP:(s + 1) * GROUP, :] = x_s.T.astype(BF16).reshape(SG, GROUP, cols)

    def pair_rows(gp):
        return slice(gp * cols, (gp + 1) * cols)

    for gp in range(SP):
        lt_pair = lt_ref[2 * gp:2 * gp + 2].reshape(2 * CW, cols)
        s = lax.dot_general(lt_pair, w_ref[gp], (((0,), (0,)), ((), ())), preferred_element_type=F32)
        sre_ref[pair_rows(gp), :] = s[:, :128]
        sim_ref[pair_rows(gp), :] = s[:, 128:]

    a_re, a_im = are_ref[...], aim_ref[...]

    def scan_step(n, c):
        rows = pl.ds(n, SP * bsz, stride=ct)
        s_re, s_im = sre_ref[rows, :], sim_ref[rows, :]
        x_re, x_im = xr_ref[...], xi_ref[...]
        sre_ref[rows, :] = x_re
        sim_ref[rows, :] = x_im
        xr_ref[...] = a_re * x_re - a_im * x_im + s_re
        xi_ref[...] = a_re * x_im + a_im * x_re + s_im
        return c

    lax.fori_loop(0, ct, scan_step, 0, unroll=4)

    for gp in range(SP):
        x_in = jnp.concatenate([sre_ref[pair_rows(gp), :], sim_ref[pair_rows(gp), :]],
                               axis=1).astype(BF16)
        for hh in range(2):
            g = 2 * gp + hh
            yt_ref[g] = _dot(tt_ref[g], lt_ref[g]) + lax.dot_general(
                vt_ref[g], x_in, (((1,), (1,)), ((), ())), preferred_element_type=F32)

    for t in range(CHUNK):
        z = yt_ref[:, t * GROUP:(t + 1) * GROUP, :].reshape(SG * GROUP, cols).T
        for b in range(bsz):
            for hf, sl in enumerate(halves):
                st_ref[b, hf, pl.ds(t, ct, stride=CHUNK), :] = z[b * ct:(b + 1) * ct, sl]
    for b in range(bsz):
        pre = jnp.concatenate([st_ref[b, 0], st_ref[b, 1]], axis=1)
        u = jnp.concatenate([uf_ref[b, 0], uf_ref[b, 1]], axis=1)
        y_ref[b] = jax.nn.gelu(pre + d_ref[...] * u).astype(BF16)


def _ssm(proj3, tt_mat, w_mat, vt_mat, d_skip, a_re, a_im, bsz, seq, ct):
    tok = ct * CHUNK
    cols = bsz * ct
    kern = functools.partial(_ssm_kernel, bsz=bsz, ct=ct)
    ucol = TILE_U * SSM_SPLIT
    return pl.pallas_call(
        kern,
        grid=(SSM_SPLIT, seq // tok),
        in_specs=[
            pl.BlockSpec((bsz, tok, CW), lambda q, t: (0, t, ucol + q)),
            pl.BlockSpec((SG, CW, CW), lambda q, t: (q, 0, 0)),
            pl.BlockSpec((SP, 2 * CW, CW), lambda q, t: (q, 0, 0)),
            pl.BlockSpec((SG, CW, CW), lambda q, t: (q, 0, 0)),
            pl.BlockSpec((1, CW), lambda q, t: (0, q)),
            pl.BlockSpec((None, SP * bsz, 128), lambda q, t: (q, 0, 0)),
            pl.BlockSpec((None, SP * bsz, 128), lambda q, t: (q, 0, 0)),
        ],
        out_specs=pl.BlockSpec((bsz, tok, CW), lambda q, t: (0, t, q)),
        out_shape=jax.ShapeDtypeStruct((bsz, seq, D_SSM), BF16),
        scratch_shapes=[
            pltpu.VMEM((bsz, 2, tok, 128), F32),
            pltpu.VMEM((SG, CW, cols), BF16),
            pltpu.VMEM((SP * cols, 128), F32),
            pltpu.VMEM((SP * cols, 128), F32),
            pltpu.VMEM((SG, CW, cols), F32),
            pltpu.VMEM((bsz, 2, tok, 128), F32),
            pltpu.VMEM((SP * bsz, 128), F32),
            pltpu.VMEM((SP * bsz, 128), F32),
        ],
        compiler_params=pltpu.CompilerParams(
            dimension_semantics=("arbitrary", "arbitrary"), vmem_limit_bytes=VMEM_LIMIT),
        name="ssm_scan",
    )(proj3, tt_mat, w_mat, vt_mat, d_skip, a_re, a_im)


def _chunk_decay_kernel(ldt_ref, lre_ref, lim_ref, are_ref, aim_ref):
    dt = jnp.exp(ldt_ref[...])
    mag = jnp.exp(float(CHUNK) * (lre_ref[...] * dt))
    ang = float(CHUNK) * (lim_ref[...] * dt)
    are_ref[...] = mag * jnp.cos(ang)
    aim_ref[...] = mag * jnp.sin(ang)


def _chunk_decay(ldt_p, lre_p, lim_p):
    shape = jax.ShapeDtypeStruct(ldt_p.shape, F32)
    return pl.pallas_call(_chunk_decay_kernel, out_shape=[shape, shape], name="ssm_chunk_decay")(
        ldt_p, lre_p, lim_p)


def _merge_kernel(x_ref, ya_ref, yg_ref, zs_ref, ga_ref, gs_ref, wg_ref, bg_ref,
                  wpa_ref, wps_ref, wo_ref, o_ref):
    yg = yg_ref[...]
    glu = _dot(yg, wg_ref[...]) + bg_ref[...]
    y_ssm = (yg.astype(F32) * jax.nn.sigmoid(glu) * zs_ref[...].astype(F32)).astype(BF16)
    merged = (ga_ref[...].astype(F32) * _dot(ya_ref[...], wpa_ref[...])
              + gs_ref[...].astype(F32) * _dot(y_ssm, wps_ref[...]))
    o_ref[...] = x_ref[...] + _dot(merged.astype(BF16), wo_ref[...])


def _merge(x2, y_att, yg, proj, w_glu, b_glu, w_pa, w_ps, w_out, tm):
    tokens = x2.shape[0]
    const = lambda i: (0, 0)
    once = pl.Buffered(1)
    return pl.pallas_call(
        _merge_kernel,
        grid=(tokens // tm,),
        in_specs=[
            pl.BlockSpec((tm, D_MODEL), lambda i: (i, 0)),
            pl.BlockSpec((tm, D_ATT), lambda i: (i, 0)),
            pl.BlockSpec((tm, D_SSM), lambda i: (i, 0)),
            pl.BlockSpec((tm, D_SSM), lambda i: (i, TILE_ZSSM)),
            pl.BlockSpec((tm, D_MODEL), lambda i: (i, TILE_GATT0 // 2)),
            pl.BlockSpec((tm, D_MODEL), lambda i: (i, TILE_GSSM0 // 2)),
            pl.BlockSpec((D_SSM, D_SSM), const, pipeline_mode=once),
            pl.BlockSpec((1, D_SSM), const, pipeline_mode=once),
            pl.BlockSpec((D_ATT, D_MODEL), const, pipeline_mode=once),
            pl.BlockSpec((D_SSM, D_MODEL), const, pipeline_mode=once),
            pl.BlockSpec((D_MODEL, D_MODEL), const, pipeline_mode=once),
        ],
        out_specs=pl.BlockSpec((tm, D_MODEL), lambda i: (i, 0)),
        out_shape=jax.ShapeDtypeStruct((tokens, D_MODEL), F32),
        compiler_params=pltpu.CompilerParams(
            dimension_semantics=("arbitrary",), vmem_limit_bytes=VMEM_LIMIT),
        name="merge_out",
    )(x2, y_att, yg, proj, proj, proj, w_glu, b_glu, w_pa, w_ps, w_out)


def kernel(x, ln_gain, w_in, q_norm_gain, k_norm_gain, lambda_q1, lambda_k1, lambda_q2, lambda_k2,
           subln_gain, ssm_lambda_re, ssm_lambda_im, ssm_log_dt, ssm_b_re, ssm_b_im, ssm_c_re,
           ssm_c_im, ssm_d, w_glu, b_glu, w_proj_att, w_proj_ssm, w_out):
    bsz, seq, _ = x.shape
    assert ln_gain.shape[0] == 1 and x.shape[2] == D_MODEL and w_in.shape[2] == N_IN
    tokens = bsz * seq
    tm_in = min(1024, seq)
    tq = min(1024, seq)
    tm_out = min(512, seq)
    ct = 128 // bsz
    assert 128 % bsz == 0 and ct % 8 == 0 and seq % (ct * CHUNK) == 0

    x2 = x.reshape(tokens, D_MODEL)
    scale = DQK ** -0.5
    q_gain = (jnp.tile(q_norm_gain[0], 2 * HEADS) * scale).reshape(1, 1024)
    k_gain = jnp.tile(k_norm_gain[0], 2 * HEADS).reshape(1, 1024)
    proj, vt = _in_proj(x2, ln_gain[0].reshape(1, D_MODEL), w_in[0].astype(BF16), q_gain, k_gain,
                        bsz, seq, tm_in)
    proj3 = proj.reshape(bsz, seq, N_IN)

    lam_params = jnp.stack([lambda_q1[0], lambda_k1[0], lambda_q2[0], lambda_k2[0]])
    subln = (subln_gain[0] * (1.0 - LAMBDA_INIT)).reshape(1, DV)
    score_bound = 1.02 * 8.0 * jnp.max(jnp.abs(q_norm_gain[0] * k_norm_gain[0]))
    attend = functools.partial(_attention, bsz=bsz, seq=seq, tq=tq)
    y_att = lax.cond(score_bound <= SCORE_BOUND_FAST,
                     functools.partial(attend, fast=True), functools.partial(attend, fast=False),
                     lam_params, subln, proj3, vt)

    tt_mat, w_mat, vt_mat = _ssm_params(ssm_log_dt[0], ssm_lambda_re[0], ssm_lambda_im[0],
                                        ssm_b_re[0], ssm_b_im[0], ssm_c_re[0], ssm_c_im[0])
    per_row = lambda a: jnp.repeat(a.reshape(PAIRS, 128), bsz, axis=0).reshape(SSM_SPLIT, SP * bsz, 128)
    a_re, a_im = _chunk_decay(per_row(jnp.repeat(ssm_log_dt[0], STATE)), per_row(ssm_lambda_re[0]),
                              per_row(ssm_lambda_im[0]))
    yg = _ssm(proj3, tt_mat, w_mat, vt_mat, ssm_d[0].reshape(1, D_SSM), a_re, a_im, bsz, seq, ct)

    out = _merge(x2, y_att.reshape(tokens, D_ATT), yg.reshape(tokens, D_SSM), proj, w_glu[0].astype(BF16),
                 b_glu[0].reshape(1, D_SSM), w_proj_att[0].astype(BF16), w_proj_ssm[0].astype(BF16),
                 w_out[0].astype(BF16), tm_out)
    return out.reshape(bsz, seq, D_MODEL)
```

```python
import functools
import math

import jax
import jax.numpy as jnp
from jax import lax
from jax.experimental import pallas as pl
from jax.experimental.pallas import tpu as pltpu

F32 = jnp.float32
BF16 = jnp.bfloat16
HIGHEST = lax.Precision.HIGHEST

D_MODEL = 2048
HEADS = 8
DQK = 64
DV = 2 * DQK
D_ATT = HEADS * DV
D_SSM = 1024
GROUP = 16
GROUPS = D_SSM // GROUP
PAIRS = GROUPS // 2
STATE = 64
N_IN = 6 * 1024 + 2 * D_MODEL
RMS_EPS = 1e-6
LAMBDA_INIT = 0.8 - 0.6 * math.exp(-0.3 * 0)
CHUNK = 16
CW = CHUNK * GROUP
SSM_SPLIT = 4
PARAM_BATCH = 4
SG = GROUPS // SSM_SPLIT
SP = SG // 2
NEG = -1e30
SCORE_BOUND_FAST = 30.0
KV_GROUP = 2
DIAG_SPLIT = 4

TILE_Q, TILE_K, TILE_V, TILE_ZATT, TILE_U, TILE_ZSSM = 0, 1, 2, 3, 4, 5
TILE_GATT0, TILE_GSSM0 = 6, 8
N_TILES = N_IN // 1024

VMEM_LIMIT = 56 * 1024 * 1024


def _dot(a, b):
    return jnp.dot(a, b, preferred_element_type=F32)


def _sigmoid(x):
    return 0.5 * jnp.tanh(0.5 * x) + 0.5


def _in_proj_kernel(x_ref, ln_ref, w_ref, qg_ref, kg_ref, gsum_ref, proj_ref, vt_ref, h_ref):
    j = pl.program_id(1)

    def group_rms_norm(acc, gain):
        sq = (acc * acc).astype(BF16)
        g = gsum_ref[...]
        ms = jnp.concatenate([_dot(sq[:, c * 256:(c + 1) * 256], g) for c in range(4)],
                             axis=1) * (1.0 / DQK)
        return acc * lax.rsqrt(ms + RMS_EPS) * gain

    @pl.when(j == TILE_Q)
    def _():
        x = x_ref[...]
        ms = jnp.mean(x * x, axis=-1, keepdims=True)
        h = (x * lax.rsqrt(ms + RMS_EPS) * ln_ref[...]).astype(BF16)
        h_ref[...] = h
        proj_ref[...] = group_rms_norm(_dot(h, w_ref[...]), qg_ref[...]).astype(BF16)

    @pl.when(j == TILE_K)
    def _():
        proj_ref[...] = group_rms_norm(_dot(h_ref[...], w_ref[...]), kg_ref[...]).astype(BF16)

    @pl.when(j == TILE_V)
    def _():
        acc = _dot(h_ref[...], w_ref[...])
        proj_ref[...] = acc.astype(BF16)
        vt_ref[...] = acc.T.astype(BF16)

    @pl.when(j == TILE_U)
    def _():
        proj_ref[...] = _dot(h_ref[...], w_ref[...]).astype(BF16)

    @pl.when((j == TILE_ZATT) | (j == TILE_ZSSM))
    def _():
        acc = _dot(h_ref[...], w_ref[...])
        proj_ref[...] = (acc * _sigmoid(acc)).astype(BF16)

    @pl.when(j >= TILE_GATT0)
    def _():
        proj_ref[...] = _sigmoid(_dot(h_ref[...], w_ref[...])).astype(BF16)


def _in_proj(x2, ln_gain, w_in, q_gain, k_gain, bsz, seq, tm):
    tokens = bsz * seq
    tiles_per_seq = seq // tm
    gi = lax.broadcasted_iota(jnp.int32, (256, 256), 0) // DQK
    gj = lax.broadcasted_iota(jnp.int32, (256, 256), 1) // DQK
    gsum = (gi == gj).astype(BF16)
    return pl.pallas_call(
        _in_proj_kernel,
        grid=(tokens // tm, N_TILES),
        in_specs=[
            pl.BlockSpec((tm, D_MODEL), lambda i, j: (i, 0)),
            pl.BlockSpec((1, D_MODEL), lambda i, j: (0, 0)),
            pl.BlockSpec((D_MODEL, 1024), lambda i, j: (0, j)),
            pl.BlockSpec((1, 1024), lambda i, j: (0, 0)),
            pl.BlockSpec((1, 1024), lambda i, j: (0, 0)),
            pl.BlockSpec((256, 256), lambda i, j: (0, 0)),
        ],
        out_specs=[
            pl.BlockSpec((tm, 1024), lambda i, j: (i, j)),
            pl.BlockSpec((None, D_ATT, tm), lambda i, j: (i // tiles_per_seq, 0, i % tiles_per_seq)),
        ],
        out_shape=[
            jax.ShapeDtypeStruct((tokens, N_IN), BF16),
            jax.ShapeDtypeStruct((bsz, D_ATT, seq), BF16),
        ],
        scratch_shapes=[pltpu.VMEM((tm, D_MODEL), BF16)],
        compiler_params=pltpu.CompilerParams(
            dimension_semantics=("arbitrary", "arbitrary"), vmem_limit_bytes=VMEM_LIMIT),
        name="in_proj",
    )(x2, ln_gain, w_in, q_gain, k_gain, gsum)


def _attn_kernel(lp_ref, sg_ref, q_ref, k_ref, vt_ref, z_ref, o_ref,
                 qbd_ref, m_ref, l_ref, acc_ref, *, seq, tq, fast):
    lp = lp_ref[...]
    lam = (jnp.exp(jnp.sum(lp[0:1] * lp[1:2], axis=-1, keepdims=True))
           - jnp.exp(jnp.sum(lp[2:3] * lp[3:4], axis=-1, keepdims=True)) + LAMBDA_INIT)

    w = tq // DIAG_SPLIT

    def q_tile(i, carry):
        q0 = pl.multiple_of(i * tq, tq)
        qt = q_ref[pl.ds(q0, tq), :].astype(F32).T
        row = lax.broadcasted_iota(jnp.int32, (DV, w), 0)
        for c in range(DIAG_SPLIT):
            qc = qt[:, c * w:(c + 1) * w]
            qbd_ref[:, 2 * c * w:(2 * c + 1) * w] = jnp.where(row < DQK, qc, 0.0).astype(BF16)
            qbd_ref[:, (2 * c + 1) * w:(2 * c + 2) * w] = jnp.where(row >= DQK, qc, 0.0).astype(BF16)
        m_ref[...] = jnp.full(m_ref.shape, NEG, F32)
        l_ref[...] = jnp.zeros(l_ref.shape, F32)
        acc_ref[...] = jnp.zeros(acc_ref.shape, F32)

        def update(s, vt, state):
            m, l, acc = state
            if fast:
                p = jnp.exp(s)
                return m, l + jnp.sum(p, axis=0, keepdims=True), acc + _dot(vt, p.astype(BF16))
            m_new = jnp.maximum(m, jnp.max(s, axis=0, keepdims=True))
            alpha = jnp.exp(m - m_new)
            p = jnp.exp(s - m_new)
            return (m_new, alpha * l + jnp.sum(p, axis=0, keepdims=True),
                    alpha * acc + _dot(vt, p.astype(BF16)))

        def block(j, state):
            k0 = pl.multiple_of(j * tq, tq)
            s = _dot(k_ref[pl.ds(k0, tq), :], qbd_ref[...])
            return update(s, vt_ref[:, pl.ds(k0, tq)], state)

        def diagonal(state):
            kpos = lax.broadcasted_iota(jnp.int32, (w, 2 * w), 0)
            qpos = lax.broadcasted_iota(jnp.int32, (w, 2 * w), 1)
            causal = kpos <= jnp.where(qpos >= w, qpos - w, qpos)
            for d in range(DIAG_SPLIT):
                c0 = d * 2 * w
                k0 = pl.multiple_of(q0 + d * w, w)
                s = _dot(k_ref[pl.ds(k0, w), :], qbd_ref[:, c0:])
                tri = jnp.where(causal, s[:, :2 * w], NEG)
                s = tri if d == DIAG_SPLIT - 1 else jnp.concatenate([tri, s[:, 2 * w:]], axis=1)
                new = update(s, vt_ref[:, pl.ds(k0, w)], tuple(a[:, c0:] for a in state))
                state = tuple(n if c0 == 0 else jnp.concatenate([a[:, :c0], n], axis=1)
                              for a, n in zip(state, new))
            return state

        def load_state():
            return m_ref[...], l_ref[...], acc_ref[...]

        def group(jj, c):
            state = load_state()
            for u in range(KV_GROUP):
                state = block(jj * KV_GROUP + u, state)
            m_ref[...], l_ref[...], acc_ref[...] = state
            return c

        n_groups = lax.shift_right_logical(i, KV_GROUP.bit_length() - 1)
        lax.fori_loop(0, n_groups, group, 0)

        for rem in range(KV_GROUP):
            @pl.when((i & (KV_GROUP - 1)) == rem)
            def _():
                state = load_state()
                for u in range(rem):
                    state = block(n_groups * KV_GROUP + u, state)
                _, l, acc = diagonal(state)
                o = acc * (1.0 / l)
                a = jnp.concatenate(
                    [o[:, 2 * c * w:(2 * c + 1) * w] - lam * o[:, (2 * c + 1) * w:(2 * c + 2) * w]
                     for c in range(DIAG_SPLIT)], axis=1)
                ms = jnp.mean(a * a, axis=0, keepdims=True)
                n = (a * lax.rsqrt(ms + RMS_EPS)).T
                out = n * sg_ref[...] * z_ref[pl.ds(q0, tq), :].astype(F32)
                o_ref[pl.ds(q0, tq), :] = out.astype(BF16)
        return carry

    lax.fori_loop(0, seq // tq, q_tile, 0)


def _attention(lam_params, subln, proj3, vt, *, bsz, seq, tq, fast):
    kern = functools.partial(_attn_kernel, seq=seq, tq=tq, fast=fast)
    qcol, kcol, zcol = TILE_Q * HEADS, TILE_K * HEADS, TILE_ZATT * HEADS
    return pl.pallas_call(
        kern,
        grid=(bsz, HEADS),
        in_specs=[
            pl.BlockSpec((4, DQK), lambda b, h: (0, 0)),
            pl.BlockSpec((1, DV), lambda b, h: (0, 0)),
            pl.BlockSpec((None, seq, DV), lambda b, h: (b, 0, qcol + h)),
            pl.BlockSpec((None, seq, DV), lambda b, h: (b, 0, kcol + h)),
            pl.BlockSpec((None, DV, seq), lambda b, h: (b, h, 0)),
            pl.BlockSpec((None, seq, DV), lambda b, h: (b, 0, zcol + h)),
        ],
        out_specs=pl.BlockSpec((None, seq, DV), lambda b, h: (b, 0, h)),
        out_shape=jax.ShapeDtypeStruct((bsz, seq, D_ATT), BF16),
        scratch_shapes=[
            pltpu.VMEM((DV, 2 * tq), BF16),
            pltpu.VMEM((1, 2 * tq), F32),
            pltpu.VMEM((1, 2 * tq), F32),
            pltpu.VMEM((DV, 2 * tq), F32),
        ],
        compiler_params=pltpu.CompilerParams(
            dimension_semantics=("arbitrary", "arbitrary"), vmem_limit_bytes=VMEM_LIMIT),
        name="diff_attention",
    )(lam_params, subln, proj3, proj3, vt, proj3)


def _ssm_param_kernel(ldt_ref, lre_ref, lim_ref, lrec_ref, limc_ref, btre_ref, btim_ref,
                      c4re_ref, c4im_ref, bpre_ref, bpim_ref, cre_ref, cim_ref,
                      tt_ref, w_ref, vt_ref):
    for gi in range(PARAM_BATCH):
        _ssm_param_group(
            gi % 2, ldt_ref.at[gi], lre_ref.at[gi], lim_ref.at[gi], lrec_ref.at[gi], limc_ref.at[gi],
            btre_ref.at[gi], btim_ref.at[gi], c4re_ref.at[gi], c4im_ref.at[gi], bpre_ref.at[gi],
            bpim_ref.at[gi], cre_ref.at[gi], cim_ref.at[gi],
            tt_ref.at[gi], w_ref.at[gi // 2, pl.ds((gi % 2) * CW, CW)], vt_ref.at[gi])


def _ssm_param_group(h, ldt_ref, lre_ref, lim_ref, lrec_ref, limc_ref, btre_ref, btim_ref,
                     c4re_ref, c4im_ref, bpre_ref, bpim_ref, cre_ref, cim_ref, tt_ref, w_ref, vt_ref):
    dt = jnp.exp(ldt_ref[...])

    def cis_pow(n, lr_, li_):
        mag = jnp.exp(n * (lr_ * dt))
        ang = n * (li_ * dt)
        return mag * jnp.cos(ang), mag * jnp.sin(ang)

    def zoh(lr_, li_):
        a_re, a_im = cis_pow(1.0, lr_, li_)
        nr, ni = a_re - 1.0, a_im
        den = lr_ * lr_ + li_ * li_
        return (nr * lr_ + ni * li_) / den, (ni * lr_ - nr * li_) / den

    def complex_mul(x_re, x_im, y_re, y_im):
        return x_re * y_re - x_im * y_im, x_re * y_im + x_im * y_re

    lr, li = lre_ref[...], lim_ref[...]
    coef_re, coef_im = zoh(lr, li)
    bb_re, bb_im = complex_mul(coef_re, coef_im, btre_ref[...], btim_ref[...])
    n_rows = jnp.minimum(lax.broadcasted_iota(jnp.int32, (24, CW), 0), CHUNK).astype(F32)
    p_re, p_im = cis_pow(n_rows, lr, li)
    c4_re, c4_im = c4re_ref[...], c4im_ref[...]
    slot = lax.broadcasted_iota(jnp.int32, (GROUP, CW), 1) // STATE

    def to_slots(v_re, v_im):
        return jnp.where(slot == h, v_re, jnp.where(slot == h + 2, v_im, 0.0)).astype(BF16)

    for s in range(CHUNK):
        n = CHUNK - 1 - s
        w_re, w_im = complex_mul(bb_re, bb_im, p_re[n:n + 1], p_im[n:n + 1])
        w_ref[s * GROUP:(s + 1) * GROUP, :] = to_slots(w_re, w_im)
        v_re, v_im = complex_mul(c4_re, c4_im, p_re[s + 1:s + 2], p_im[s + 1:s + 2])
        vt_ref[s * GROUP:(s + 1) * GROUP, :] = to_slots(v_re, -v_im)

    lrc, lic = lrec_ref[...], limc_ref[...]
    cc_re, cc_im = zoh(lrc, lic)
    bp_re, bp_im = complex_mul(cc_re, cc_im, bpre_ref[...], bpim_ref[...])
    n_lanes = jnp.minimum(lax.broadcasted_iota(jnp.int32, (STATE, 128), 1), CHUNK).astype(F32)
    q_re, q_im = cis_pow(n_lanes, lrc, lic)
    sel = lax.broadcasted_iota(jnp.int32, (128, CW), 0)
    lane_r = lax.broadcasted_iota(jnp.int32, (128, CW), 1) // GROUP
    lane_ci = lax.broadcasted_iota(jnp.int32, (128, CW), 1) % GROUP
    e_pow = (sel == CHUNK - 1 - lane_r).astype(F32)
    e_ci = (sel == lane_ci).astype(F32)

    def hdot(a, b):
        return jnp.dot(a, b, preferred_element_type=F32, precision=HIGHEST)

    r_re, r_im = complex_mul(hdot(q_re, e_pow), hdot(q_im, e_pow), hdot(bp_re, e_ci), hdot(bp_im, e_ci))
    krev = hdot(cre_ref[...], r_re) - hdot(cim_ref[...], r_im)
    lane_s = lax.broadcasted_iota(jnp.int32, (GROUP, CW), 1) // GROUP
    for t in range(CHUNK):
        shift = (CHUNK - 1 - t) * GROUP
        moved = krev if shift == 0 else pltpu.roll(krev, CW - shift, axis=1)
        tt_ref[t * GROUP:(t + 1) * GROUP, :] = jnp.where(lane_s <= t, moved, 0.0).astype(BF16)


def _ssm_params(log_dt, lam_re, lam_im, b_re, b_im, c_re, c_im):
    tile4 = lambda a: jnp.tile(a, (1, 1, 4))
    pad128 = lambda a: jnp.pad(a, ((0, 0), (0, 0), (0, 128 - GROUP)))
    ldt = log_dt.reshape(GROUPS, 1, 1)
    lre4 = tile4(lam_re.reshape(GROUPS, 1, STATE))
    lim4 = tile4(lam_im.reshape(GROUPS, 1, STATE))
    lrec = lam_re.reshape(GROUPS, STATE, 1)
    limc = lam_im.reshape(GROUPS, STATE, 1)
    bt_re = tile4(jnp.swapaxes(b_re, 1, 2))
    bt_im = tile4(jnp.swapaxes(b_im, 1, 2))
    c4_re, c4_im = tile4(c_re), tile4(c_im)
    bp_re, bp_im = pad128(b_re), pad128(b_im)

    def spec(shape):
        return pl.BlockSpec((PARAM_BATCH,) + shape, lambda g: (g, 0, 0))

    return pl.pallas_call(
        _ssm_param_kernel,
        grid=(GROUPS // PARAM_BATCH,),
        in_specs=[spec((1, 1)), spec((1, CW)), spec((1, CW)), spec((STATE, 1)), spec((STATE, 1)),
                  spec((GROUP, CW)), spec((GROUP, CW)), spec((GROUP, CW)), spec((GROUP, CW)),
                  spec((STATE, 128)), spec((STATE, 128)), spec((GROUP, STATE)), spec((GROUP, STATE))],
        out_specs=[
            pl.BlockSpec((PARAM_BATCH, CW, CW), lambda g: (g, 0, 0)),
            pl.BlockSpec((PARAM_BATCH // 2, 2 * CW, CW), lambda g: (g, 0, 0)),
            pl.BlockSpec((PARAM_BATCH, CW, CW), lambda g: (g, 0, 0)),
        ],
        out_shape=[
            jax.ShapeDtypeStruct((GROUPS, CW, CW), BF16),
            jax.ShapeDtypeStruct((PAIRS, 2 * CW, CW), BF16),
            jax.ShapeDtypeStruct((GROUPS, CW, CW), BF16),
        ],
        compiler_params=pltpu.CompilerParams(dimension_semantics=("arbitrary",)),
        name="ssm_params",
    )(ldt, lre4, lim4, lrec, limc, bt_re, bt_im, c4_re, c4_im, bp_re, bp_im, c_re, c_im)


def _ssm_kernel(u_ref, tt_ref, w_ref, vt_ref, d_ref, are_ref, aim_ref, y_ref,
                uf_ref, lt_ref, sre_ref, sim_ref, yt_ref, st_ref, xr_ref, xi_ref, *, bsz, ct):
    cols = bsz * ct
    halves = [slice(0, 128), slice(128, 256)]

    @pl.when(pl.program_id(1) == 0)
    def _():
        xr_ref[...] = jnp.zeros(xr_ref.shape, F32)
        xi_ref[...] = jnp.zeros(xi_ref.shape, F32)

    for b in range(bsz):
        u = u_ref[b].astype(F32)
        for hf, sl in enumerate(halves):
            uf_ref[b, hf] = u[:, sl]
    for s in range(CHUNK):
        x_s = jnp.concatenate(
            [jnp.concatenate([uf_ref[b, hf, pl.ds(s, ct, stride=CHUNK), :] for b in range(bsz)], axis=0)
             for hf in range(2)], axis=1)
        lt_ref[:, s * GROUP:(s + 1) * GROUP, :] = x_s.T.astype(BF16).reshape(SG, GROUP, cols)

    def pair_rows(gp):
        return slice(gp * cols, (gp + 1) * cols)

    for gp in range(SP):
        lt_pair = lt_ref[2 * gp:2 * gp + 2].reshape(2 * CW, cols)
        s = lax.dot_general(lt_pair, w_ref[gp], (((0,), (0,)), ((), ())), preferred_element_type=F32)
        sre_ref[pair_rows(gp), :] = s[:, :128]
        sim_ref[pair_rows(gp), :] = s[:, 128:]

    a_re, a_im = are_ref[...], aim_ref[...]

    def scan_step(n, c):
        rows = pl.ds(n, SP * bsz, stride=ct)
        s_re, s_im = sre_ref[rows, :], sim_ref[rows, :]
        x_re, x_im = xr_ref[...], xi_ref[...]
        sre_ref[rows, :] = x_re
        sim_ref[rows, :] = x_im
        xr_ref[...] = a_re * x_re - a_im * x_im + s_re
        xi_ref[...] = a_re * x_im + a_im * x_re + s_im
        return c

    lax.fori_loop(0, ct, scan_step, 0, unroll=4)

    for gp in range(SP):
        x_in = jnp.concatenate([sre_ref[pair_rows(gp), :], sim_ref[pair_rows(gp), :]],
                               axis=1).astype(BF16)
        for hh in range(2):
            g = 2 * gp + hh
            yt_ref[g] = _dot(tt_ref[g], lt_ref[g]) + lax.dot_general(
                vt_ref[g], x_in, (((1,), (1,)), ((), ())), preferred_element_type=F32)

    for t in range(CHUNK):
        z = yt_ref[:, t * GROUP:(t + 1) * GROUP, :].reshape(SG * GROUP, cols).T
        for b in range(bsz):
            for hf, sl in enumerate(halves):
                st_ref[b, hf, pl.ds(t, ct, stride=CHUNK), :] = z[b * ct:(b + 1) * ct, sl]
    for b in range(bsz):
        pre = jnp.concatenate([st_ref[b, 0], st_ref[b, 1]], axis=1)
        u = jnp.concatenate([uf_ref[b, 0], uf_ref[b, 1]], axis=1)
        y_ref[b] = jax.nn.gelu(pre + d_ref[...] * u).astype(BF16)


def _ssm(proj3, tt_mat, w_mat, vt_mat, d_skip, a_re, a_im, bsz, seq, ct):
    tok = ct * CHUNK
    cols = bsz * ct
    kern = functools.partial(_ssm_kernel, bsz=bsz, ct=ct)
    ucol = TILE_U * SSM_SPLIT
    return pl.pallas_call(
        kern,
        grid=(SSM_SPLIT, seq // tok),
        in_specs=[
            pl.BlockSpec((bsz, tok, CW), lambda q, t: (0, t, ucol + q)),
            pl.BlockSpec((SG, CW, CW), lambda q, t: (q, 0, 0)),
            pl.BlockSpec((SP, 2 * CW, CW), lambda q, t: (q, 0, 0)),
            pl.BlockSpec((SG, CW, CW), lambda q, t: (q, 0, 0)),
            pl.BlockSpec((1, CW), lambda q, t: (0, q)),
            pl.BlockSpec((None, SP * bsz, 128), lambda q, t: (q, 0, 0)),
            pl.BlockSpec((None, SP * bsz, 128), lambda q, t: (q, 0, 0)),
        ],
        out_specs=pl.BlockSpec((bsz, tok, CW), lambda q, t: (0, t, q)),
        out_shape=jax.ShapeDtypeStruct((bsz, seq, D_SSM), BF16),
        scratch_shapes=[
            pltpu.VMEM((bsz, 2, tok, 128), F32),
            pltpu.VMEM((SG, CW, cols), BF16),
            pltpu.VMEM((SP * cols, 128), F32),
            pltpu.VMEM((SP * cols, 128), F32),
            pltpu.VMEM((SG, CW, cols), F32),
            pltpu.VMEM((bsz, 2, tok, 128), F32),
            pltpu.VMEM((SP * bsz, 128), F32),
            pltpu.VMEM((SP * bsz, 128), F32),
        ],
        compiler_params=pltpu.CompilerParams(
            dimension_semantics=("arbitrary", "arbitrary"), vmem_limit_bytes=VMEM_LIMIT),
        name="ssm_scan",
    )(proj3, tt_mat, w_mat, vt_mat, d_skip, a_re, a_im)


def _chunk_decay_kernel(ldt_ref, lre_ref, lim_ref, are_ref, aim_ref):
    dt = jnp.exp(ldt_ref[...])
    mag = jnp.exp(float(CHUNK) * (lre_ref[...] * dt))
    ang = float(CHUNK) * (lim_ref[...] * dt)
    are_ref[...] = mag * jnp.cos(ang)
    aim_ref[...] = mag * jnp.sin(ang)


def _chunk_decay(ldt_p, lre_p, lim_p):
    shape = jax.ShapeDtypeStruct(ldt_p.shape, F32)
    return pl.pallas_call(_chunk_decay_kernel, out_shape=[shape, shape], name="ssm_chunk_decay")(
        ldt_p, lre_p, lim_p)


def _merge_kernel(x_ref, ya_ref, yg_ref, zs_ref, ga_ref, gs_ref, wg_ref, bg_ref,
                  wpa_ref, wps_ref, wo_ref, o_ref):
    yg = yg_ref[...]
    glu = _dot(yg, wg_ref[...]) + bg_ref[...]
    y_ssm = (yg.astype(F32) * jax.nn.sigmoid(glu) * zs_ref[...].astype(F32)).astype(BF16)
    merged = (ga_ref[...].astype(F32) * _dot(ya_ref[...], wpa_ref[...])
              + gs_ref[...].astype(F32) * _dot(y_ssm, wps_ref[...]))
    o_ref[...] = x_ref[...] + _dot(merged.astype(BF16), wo_ref[...])


def _merge(x2, y_att, yg, proj, w_glu, b_glu, w_pa, w_ps, w_out, tm):
    tokens = x2.shape[0]
    const = lambda i: (0, 0)
    once = pl.Buffered(1)
    return pl.pallas_call(
        _merge_kernel,
        grid=(tokens // tm,),
        in_specs=[
            pl.BlockSpec((tm, D_MODEL), lambda i: (i, 0)),
            pl.BlockSpec((tm, D_ATT), lambda i: (i, 0)),
            pl.BlockSpec((tm, D_SSM), lambda i: (i, 0)),
            pl.BlockSpec((tm, D_SSM), lambda i: (i, TILE_ZSSM)),
            pl.BlockSpec((tm, D_MODEL), lambda i: (i, TILE_GATT0 // 2)),
            pl.BlockSpec((tm, D_MODEL), lambda i: (i, TILE_GSSM0 // 2)),
            pl.BlockSpec((D_SSM, D_SSM), const, pipeline_mode=once),
            pl.BlockSpec((1, D_SSM), const, pipeline_mode=once),
            pl.BlockSpec((D_ATT, D_MODEL), const, pipeline_mode=once),
            pl.BlockSpec((D_SSM, D_MODEL), const, pipeline_mode=once),
            pl.BlockSpec((D_MODEL, D_MODEL), const, pipeline_mode=once),
        ],
        out_specs=pl.BlockSpec((tm, D_MODEL), lambda i: (i, 0)),
        out_shape=jax.ShapeDtypeStruct((tokens, D_MODEL), F32),
        compiler_params=pltpu.CompilerParams(
            dimension_semantics=("arbitrary",), vmem_limit_bytes=VMEM_LIMIT),
        name="merge_out",
    )(x2, y_att, yg, proj, proj, proj, w_glu, b_glu, w_pa, w_ps, w_out)


def kernel(x, ln_gain, w_in, q_norm_gain, k_norm_gain, lambda_q1, lambda_k1, lambda_q2, lambda_k2,
           subln_gain, ssm_lambda_re, ssm_lambda_im, ssm_log_dt, ssm_b_re, ssm_b_im, ssm_c_re,
           ssm_c_im, ssm_d, w_glu, b_glu, w_proj_att, w_proj_ssm, w_out):
    bsz, seq, _ = x.shape
    assert ln_gain.shape[0] == 1 and x.shape[2] == D_MODEL and w_in.shape[2] == N_IN
    tokens = bsz * seq
    tm_in = min(1024, seq)
    tq = min(1024, seq)
    tm_out = min(512, seq)
    ct = 128 // bsz
    assert 128 % bsz == 0 and ct % 8 == 0 and seq % (ct * CHUNK) == 0

    x2 = x.reshape(tokens, D_MODEL)
    scale = DQK ** -0.5
    q_gain = (jnp.tile(q_norm_gain[0], 2 * HEADS) * scale).reshape(1, 1024)
    k_gain = jnp.tile(k_norm_gain[0], 2 * HEADS).reshape(1, 1024)
    proj, vt = _in_proj(x2, ln_gain[0].reshape(1, D_MODEL), w_in[0].astype(BF16), q_gain, k_gain,
                        bsz, seq, tm_in)
    proj3 = proj.reshape(bsz, seq, N_IN)

    lam_params = jnp.stack([lambda_q1[0], lambda_k1[0], lambda_q2[0], lambda_k2[0]])
    subln = (subln_gain[0] * (1.0 - LAMBDA_INIT)).reshape(1, DV)
    score_bound = 1.02 * 8.0 * jnp.max(jnp.abs(q_norm_gain[0] * k_norm_gain[0]))
    attend = functools.partial(_attention, bsz=bsz, seq=seq, tq=tq)
    y_att = lax.cond(score_bound <= SCORE_BOUND_FAST,
                     functools.partial(attend, fast=True), functools.partial(attend, fast=False),
                     lam_params, subln, proj3, vt)

    tt_mat, w_mat, vt_mat = _ssm_params(ssm_log_dt[0], ssm_lambda_re[0], ssm_lambda_im[0],
                                        ssm_b_re[0], ssm_b_im[0], ssm_c_re[0], ssm_c_im[0])
    per_row = lambda a: jnp.repeat(a.reshape(PAIRS, 128), bsz, axis=0).reshape(SSM_SPLIT, SP * bsz, 128)
    a_re, a_im = _chunk_decay(per_row(jnp.repeat(ssm_log_dt[0], STATE)), per_row(ssm_lambda_re[0]),
                              per_row(ssm_lambda_im[0]))
    yg = _ssm(proj3, tt_mat, w_mat, vt_mat, ssm_d[0].reshape(1, D_SSM), a_re, a_im, bsz, seq, ct)

    out = _merge(x2, y_att.reshape(tokens, D_ATT), yg.reshape(tokens, D_SSM), proj, w_glu[0].astype(BF16),
                 b_glu[0].reshape(1, D_SSM), w_proj_att[0].astype(BF16), w_proj_ssm[0].astype(BF16),
                 w_out[0].astype(BF16), tm_out)
    return out.reshape(bsz, seq, D_MODEL)
```

```python
import functools
import math

import jax
import jax.numpy as jnp
from jax import lax
from jax.experimental import pallas as pl
from jax.experimental.pallas import tpu as pltpu

F32 = jnp.float32
BF16 = jnp.bfloat16
HIGHEST = lax.Precision.HIGHEST

D_MODEL = 2048
HEADS = 8
DQK = 64
DV = 2 * DQK
D_ATT = HEADS * DV
D_SSM = 1024
GROUP = 16
GROUPS = D_SSM // GROUP
PAIRS = GROUPS // 2
STATE = 64
N_IN = 6 * 1024 + 2 * D_MODEL
RMS_EPS = 1e-6
LAMBDA_INIT = 0.8 - 0.6 * math.exp(-0.3 * 0)
CHUNK = 16
CW = CHUNK * GROUP
SSM_SPLIT = 4
PARAM_BATCH = 4
SG = GROUPS // SSM_SPLIT
SP = SG // 2
NEG = -1e30
SCORE_BOUND_FAST = 30.0
KV_GROUP = 2
DIAG_SPLIT = 4

TILE_Q, TILE_K, TILE_V, TILE_ZATT, TILE_U, TILE_ZSSM = 0, 1, 2, 3, 4, 5
TILE_GATT0, TILE_GSSM0 = 6, 8
N_TILES = N_IN // 1024

VMEM_LIMIT = 56 * 1024 * 1024


def _dot(a, b):
    return jnp.dot(a, b, preferred_element_type=F32)


def _sigmoid(x):
    return 0.5 * jnp.tanh(0.5 * x) + 0.5


def _in_proj_kernel(x_ref, ln_ref, w_ref, qg_ref, kg_ref, gsum_ref, proj_ref, vt_ref, h_ref):
    j = pl.program_id(1)

    def group_rms_norm(acc, gain):
        sq = (acc * acc).astype(BF16)
        g = gsum_ref[...]
        ms = jnp.concatenate([_dot(sq[:, c * 256:(c + 1) * 256], g) for c in range(4)],
                             axis=1) * (1.0 / DQK)
        return acc * lax.rsqrt(ms + RMS_EPS) * gain

    @pl.when(j == TILE_Q)
    def _():
        x = x_ref[...]
        ms = jnp.mean(x * x, axis=-1, keepdims=True)
        h = (x * lax.rsqrt(ms + RMS_EPS) * ln_ref[...]).astype(BF16)
        h_ref[...] = h
        proj_ref[...] = group_rms_norm(_dot(h, w_ref[...]), qg_ref[...]).astype(BF16)

    @pl.when(j == TILE_K)
    def _():
        proj_ref[...] = group_rms_norm(_dot(h_ref[...], w_ref[...]), kg_ref[...]).astype(BF16)

    @pl.when(j == TILE_V)
    def _():
        acc = _dot(h_ref[...], w_ref[...])
        proj_ref[...] = acc.astype(BF16)
        vt_ref[...] = acc.T.astype(BF16)

    @pl.when(j == TILE_U)
    def _():
        proj_ref[...] = _dot(h_ref[...], w_ref[...]).astype(BF16)

    @pl.when((j == TILE_ZATT) | (j == TILE_ZSSM))
    def _():
        acc = _dot(h_ref[...], w_ref[...])
        proj_ref[...] = (acc * _sigmoid(acc)).astype(BF16)

    @pl.when(j >= TILE_GATT0)
    def _():
        proj_ref[...] = _sigmoid(_dot(h_ref[...], w_ref[...])).astype(BF16)


def _in_proj(x2, ln_gain, w_in, q_gain, k_gain, bsz, seq, tm):
    tokens = bsz * seq
    tiles_per_seq = seq // tm
    gi = lax.broadcasted_iota(jnp.int32, (256, 256), 0) // DQK
    gj = lax.broadcasted_iota(jnp.int32, (256, 256), 1) // DQK
    gsum = (gi == gj).astype(BF16)
    return pl.pallas_call(
        _in_proj_kernel,
        grid=(tokens // tm, N_TILES),
        in_specs=[
            pl.BlockSpec((tm, D_MODEL), lambda i, j: (i, 0)),
            pl.BlockSpec((1, D_MODEL), lambda i, j: (0, 0)),
            pl.BlockSpec((D_MODEL, 1024), lambda i, j: (0, j)),
            pl.BlockSpec((1, 1024), lambda i, j: (0, 0)),
            pl.BlockSpec((1, 1024), lambda i, j: (0, 0)),
            pl.BlockSpec((256, 256), lambda i, j: (0, 0)),
        ],
        out_specs=[
            pl.BlockSpec((tm, 1024), lambda i, j: (i, j)),
            pl.BlockSpec((None, D_ATT, tm), lambda i, j: (i // tiles_per_seq, 0, i % tiles_per_seq)),
        ],
        out_shape=[
            jax.ShapeDtypeStruct((tokens, N_IN), BF16),
            jax.ShapeDtypeStruct((bsz, D_ATT, seq), BF16),
        ],
        scratch_shapes=[pltpu.VMEM((tm, D_MODEL), BF16)],
        compiler_params=pltpu.CompilerParams(
            dimension_semantics=("arbitrary", "arbitrary"), vmem_limit_bytes=VMEM_LIMIT),
        name="in_proj",
    )(x2, ln_gain, w_in, q_gain, k_gain, gsum)


def _attn_kernel(lp_ref, sg_ref, q_ref, k_ref, vt_ref, z_ref, o_ref,
                 qbd_ref, m_ref, l_ref, acc_ref, *, seq, tq, fast):
    lp = lp_ref[...]
    lam = (jnp.exp(jnp.sum(lp[0:1] * lp[1:2], axis=-1, keepdims=True))
           - jnp.exp(jnp.sum(lp[2:3] * lp[3:4], axis=-1, keepdims=True)) + LAMBDA_INIT)

    w = tq // DIAG_SPLIT

    def q_tile(i, carry):
        q0 = pl.multiple_of(i * tq, tq)
        qt = q_ref[pl.ds(q0, tq), :].astype(F32).T
        row = lax.broadcasted_iota(jnp.int32, (DV, w), 0)
        for c in range(DIAG_SPLIT):
            qc = qt[:, c * w:(c + 1) * w]
            qbd_ref[:, 2 * c * w:(2 * c + 1) * w] = jnp.where(row < DQK, qc, 0.0).astype(BF16)
            qbd_ref[:, (2 * c + 1) * w:(2 * c + 2) * w] = jnp.where(row >= DQK, qc, 0.0).astype(BF16)
        m_ref[...] = jnp.full(m_ref.shape, NEG, F32)
        l_ref[...] = jnp.zeros(l_ref.shape, F32)
        acc_ref[...] = jnp.zeros(acc_ref.shape, F32)

        def update(s, vt, state):
            m, l, acc = state
            if fast:
                p = jnp.exp(s)
                return m, l + jnp.sum(p, axis=0, keepdims=True), acc + _dot(vt, p.astype(BF16))
            m_new = jnp.maximum(m, jnp.max(s, axis=0, keepdims=True))
            alpha = jnp.exp(m - m_new)
            p = jnp.exp(s - m_new)
            return (m_new, alpha * l + jnp.sum(p, axis=0, keepdims=True),
                    alpha * acc + _dot(vt, p.astype(BF16)))

        def block(j, state):
            k0 = pl.multiple_of(j * tq, tq)
            s = _dot(k_ref[pl.ds(k0, tq), :], qbd_ref[...])
            return update(s, vt_ref[:, pl.ds(k0, tq)], state)

        def diagonal(state):
            kpos = lax.broadcasted_iota(jnp.int32, (w, 2 * w), 0)
            qpos = lax.broadcasted_iota(jnp.int32, (w, 2 * w), 1)
            causal = kpos <= jnp.where(qpos >= w, qpos - w, qpos)
            for d in range(DIAG_SPLIT):
                c0 = d * 2 * w
                k0 = pl.multiple_of(q0 + d * w, w)
                s = _dot(k_ref[pl.ds(k0, w), :], qbd_ref[:, c0:])
                tri = jnp.where(causal, s[:, :2 * w], NEG)
                s = tri if d == DIAG_SPLIT - 1 else jnp.concatenate([tri, s[:, 2 * w:]], axis=1)
                new = update(s, vt_ref[:, pl.ds(k0, w)], tuple(a[:, c0:] for a in state))
                state = tuple(n if c0 == 0 else jnp.concatenate([a[:, :c0], n], axis=1)
                              for a, n in zip(state, new))
            return state

        def load_state():
            return m_ref[...], l_ref[...], acc_ref[...]

        def group(jj, c):
            state = load_state()
            for u in range(KV_GROUP):
                state = block(jj * KV_GROUP + u, state)
            m_ref[...], l_ref[...], acc_ref[...] = state
            return c

        n_groups = lax.shift_right_logical(i, KV_GROUP.bit_length() - 1)
        lax.fori_loop(0, n_groups, group, 0)

        for rem in range(KV_GROUP):
            @pl.when((i & (KV_GROUP - 1)) == rem)
            def _():
                state = load_state()
                for u in range(rem):
                    state = block(n_groups * KV_GROUP + u, state)
                _, l, acc = diagonal(state)
                o = acc * (1.0 / l)
                a = jnp.concatenate(
                    [o[:, 2 * c * w:(2 * c + 1) * w] - lam * o[:, (2 * c + 1) * w:(2 * c + 2) * w]
                     for c in range(DIAG_SPLIT)], axis=1)
                ms = jnp.mean(a * a, axis=0, keepdims=True)
                n = (a * lax.rsqrt(ms + RMS_EPS)).T
                out = n * sg_ref[...] * z_ref[pl.ds(q0, tq), :].astype(F32)
                o_ref[pl.ds(q0, tq), :] = out.astype(BF16)
        return carry

    lax.fori_loop(0, seq // tq, q_tile, 0)


def _attention(lam_params, subln, proj3, vt, *, bsz, seq, tq, fast):
    kern = functools.partial(_attn_kernel, seq=seq, tq=tq, fast=fast)
    qcol, kcol, zcol = TILE_Q * HEADS, TILE_K * HEADS, TILE_ZATT * HEADS
    return pl.pallas_call(
        kern,
        grid=(bsz, HEADS),
        in_specs=[
            pl.BlockSpec((4, DQK), lambda b, h: (0, 0)),
            pl.BlockSpec((1, DV), lambda b, h: (0, 0)),
            pl.BlockSpec((None, seq, DV), lambda b, h: (b, 0, qcol + h)),
            pl.BlockSpec((None, seq, DV), lambda b, h: (b, 0, kcol + h)),
            pl.BlockSpec((None, DV, seq), lambda b, h: (b, h, 0)),
            pl.BlockSpec((None, seq, DV), lambda b, h: (b, 0, zcol + h)),
        ],
        out_specs=pl.BlockSpec((None, seq, DV), lambda b, h: (b, 0, h)),
        out_shape=jax.ShapeDtypeStruct((bsz, seq, D_ATT), BF16),
        scratch_shapes=[
            pltpu.VMEM((DV, 2 * tq), BF16),
            pltpu.VMEM((1, 2 * tq), F32),
            pltpu.VMEM((1, 2 * tq), F32),
            pltpu.VMEM((DV, 2 * tq), F32),
        ],
        compiler_params=pltpu.CompilerParams(
            dimension_semantics=("arbitrary", "arbitrary"), vmem_limit_bytes=VMEM_LIMIT),
        name="diff_attention",
    )(lam_params, subln, proj3, proj3, vt, proj3)


def _ssm_param_kernel(ldt_ref, lre_ref, lim_ref, lrec_ref, limc_ref, btre_ref, btim_ref,
                      c4re_ref, c4im_ref, bpre_ref, bpim_ref, cre_ref, cim_ref,
                      tt_ref, w_ref, vt_ref):
    for gi in range(PARAM_BATCH):
        _ssm_param_group(
            gi % 2, ldt_ref.at[gi], lre_ref.at[gi], lim_ref.at[gi], lrec_ref.at[gi], limc_ref.at[gi],
            btre_ref.at[gi], btim_ref.at[gi], c4re_ref.at[gi], c4im_ref.at[gi], bpre_ref.at[gi],
            bpim_ref.at[gi], cre_ref.at[gi], cim_ref.at[gi],
            tt_ref.at[gi], w_ref.at[gi // 2, pl.ds((gi % 2) * CW, CW)], vt_ref.at[gi])


def _ssm_param_group(h, ldt_ref, lre_ref, lim_ref, lrec_ref, limc_ref, btre_ref, btim_ref,
                     c4re_ref, c4im_ref, bpre_ref, bpim_ref, cre_ref, cim_ref, tt_ref, w_ref, vt_ref):
    dt = jnp.exp(ldt_ref[...])

    def cis_pow(n, lr_, li_):
        mag = jnp.exp(n * (lr_ * dt))
        ang = n * (li_ * dt)
        return mag * jnp.cos(ang), mag * jnp.sin(ang)

    def zoh(lr_, li_):
        a_re, a_im = cis_pow(1.0, lr_, li_)
        nr, ni = a_re - 1.0, a_im
        den = lr_ * lr_ + li_ * li_
        return (nr * lr_ + ni * li_) / den, (ni * lr_ - nr * li_) / den

    def complex_mul(x_re, x_im, y_re, y_im):
        return x_re * y_re - x_im * y_im, x_re * y_im + x_im * y_re

    lr, li = lre_ref[...], lim_ref[...]
    coef_re, coef_im = zoh(lr, li)
    bb_re, bb_im = complex_mul(coef_re, coef_im, btre_ref[...], btim_ref[...])
    n_rows = jnp.minimum(lax.broadcasted_iota(jnp.int32, (24, CW), 0), CHUNK).astype(F32)
    p_re, p_im = cis_pow(n_rows, lr, li)
    c4_re, c4_im = c4re_ref[...], c4im_ref[...]
    slot = lax.broadcasted_iota(jnp.int32, (GROUP, CW), 1) // STATE

    def to_slots(v_re, v_im):
        return jnp.where(slot == h, v_re, jnp.where(slot == h + 2, v_im, 0.0)).astype(BF16)

    for s in range(CHUNK):
        n = CHUNK - 1 - s
        w_re, w_im = complex_mul(bb_re, bb_im, p_re[n:n + 1], p_im[n:n + 1])
        w_ref[s * GROUP:(s + 1) * GROUP, :] = to_slots(w_re, w_im)
        v_re, v_im = complex_mul(c4_re, c4_im, p_re[s + 1:s + 2], p_im[s + 1:s + 2])
        vt_ref[s * GROUP:(s + 1) * GROUP, :] = to_slots(v_re, -v_im)

    lrc, lic = lrec_ref[...], limc_ref[...]
    cc_re, cc_im = zoh(lrc, lic)
    bp_re, bp_im = complex_mul(cc_re, cc_im, bpre_ref[...], bpim_ref[...])
    n_lanes = jnp.minimum(lax.broadcasted_iota(jnp.int32, (STATE, 128), 1), CHUNK).astype(F32)
    q_re, q_im = cis_pow(n_lanes, lrc, lic)
    sel = lax.broadcasted_iota(jnp.int32, (128, CW), 0)
    lane_r = lax.broadcasted_iota(jnp.int32, (128, CW), 1) // GROUP
    lane_ci = lax.broadcasted_iota(jnp.int32, (128, CW), 1) % GROUP
    e_pow = (sel == CHUNK - 1 - lane_r).astype(F32)
    e_ci = (sel == lane_ci).astype(F32)

    def hdot(a, b):
        return jnp.dot(a, b, preferred_element_type=F32, precision=HIGHEST)

    r_re, r_im = complex_mul(hdot(q_re, e_pow), hdot(q_im, e_pow), hdot(bp_re, e_ci), hdot(bp_im, e_ci))
    krev = hdot(cre_ref[...], r_re) - hdot(cim_ref[...], r_im)
    lane_s = lax.broadcasted_iota(jnp.int32, (GROUP, CW), 1) // GROUP
    for t in range(CHUNK):
        shift = (CHUNK - 1 - t) * GROUP
        moved = krev if shift == 0 else pltpu.roll(krev, CW - shift, axis=1)
        tt_ref[t * GROUP:(t + 1) * GROUP, :] = jnp.where(lane_s <= t, moved, 0.0).astype(BF16)


def _ssm_params(log_dt, lam_re, lam_im, b_re, b_im, c_re, c_im):
    tile4 = lambda a: jnp.tile(a, (1, 1, 4))
    pad128 = lambda a: jnp.pad(a, ((0, 0), (0, 0), (0, 128 - GROUP)))
    ldt = log_dt.reshape(GROUPS, 1, 1)
    lre4 = tile4(lam_re.reshape(GROUPS, 1, STATE))
    lim4 = tile4(lam_im.reshape(GROUPS, 1, STATE))
    lrec = lam_re.reshape(GROUPS, STATE, 1)
    limc = lam_im.reshape(GROUPS, STATE, 1)
    bt_re = tile4(jnp.swapaxes(b_re, 1, 2))
    bt_im = tile4(jnp.swapaxes(b_im, 1, 2))
    c4_re, c4_im = tile4(c_re), tile4(c_im)
    bp_re, bp_im = pad128(b_re), pad128(b_im)

    def spec(shape):
        return pl.BlockSpec((PARAM_BATCH,) + shape, lambda g: (g, 0, 0))

    return pl.pallas_call(
        _ssm_param_kernel,
        grid=(GROUPS // PARAM_BATCH,),
        in_specs=[spec((1, 1)), spec((1, CW)), spec((1, CW)), spec((STATE, 1)), spec((STATE, 1)),
                  spec((GROUP, CW)), spec((GROUP, CW)), spec((GROUP, CW)), spec((GROUP, CW)),
                  spec((STATE, 128)), spec((STATE, 128)), spec((GROUP, STATE)), spec((GROUP, STATE))],
        out_specs=[
            pl.BlockSpec((PARAM_BATCH, CW, CW), lambda g: (g, 0, 0)),
            pl.BlockSpec((PARAM_BATCH // 2, 2 * CW, CW), lambda g: (g, 0, 0)),
            pl.BlockSpec((PARAM_BATCH, CW, CW), lambda g: (g, 0, 0)),
        ],
        out_shape=[
            jax.ShapeDtypeStruct((GROUPS, CW, CW), BF16),
            jax.ShapeDtypeStruct((PAIRS, 2 * CW, CW), BF16),
            jax.ShapeDtypeStruct((GROUPS, CW, CW), BF16),
        ],
        compiler_params=pltpu.CompilerParams(dimension_semantics=("arbitrary",)),
        name="ssm_params",
    )(ldt, lre4, lim4, lrec, limc, bt_re, bt_im, c4_re, c4_im, bp_re, bp_im, c_re, c_im)


def _ssm_kernel(*refs, bsz, ct):
    u_refs = refs[:CHUNK]
    (tt_ref, w_ref, vt_ref, d_ref, are_ref, aim_ref, y_ref,
     lt_ref, sre_ref, sim_ref, ere_ref, eim_ref, yt_ref, st_ref, xr_ref, xi_ref) = refs[CHUNK:]
    cols = bsz * ct
    halves = [slice(0, 128), slice(128, 256)]

    @pl.when(pl.program_id(1) == 0)
    def _():
        xr_ref[...] = jnp.zeros(xr_ref.shape, F32)
        xi_ref[...] = jnp.zeros(xi_ref.shape, F32)

    for s in range(CHUNK):
        x_s = u_refs[s][...].reshape(cols, CW).astype(F32)
        lt_ref[:, s * GROUP:(s + 1) * GROUP, :] = x_s.T.astype(BF16).reshape(SG, GROUP, cols)

    def pair_rows(gp):
        return slice(gp * cols, (gp + 1) * cols)

    for gp in range(SP):
        lt_pair = lt_ref[2 * gp:2 * gp + 2].reshape(2 * CW, cols)
        s = lax.dot_general(lt_pair, w_ref[gp], (((0,), (0,)), ((), ())), preferred_element_type=F32)
        sre_ref[pair_rows(gp), :] = s[:, :128]
        sim_ref[pair_rows(gp), :] = s[:, 128:]

    a_re, a_im = are_ref[...], aim_ref[...]
    srows = SP * bsz

    def scan_step(n, x):
        x_re, x_im = x
        rows = pl.ds(n, srows, stride=ct)
        s_re, s_im = sre_ref[rows, :], sim_ref[rows, :]
        dense = pl.ds(pl.multiple_of(n * srows, srows), srows)
        ere_ref[dense, :] = x_re
        eim_ref[dense, :] = x_im
        return a_re * x_re - a_im * x_im + s_re, a_re * x_im + a_im * x_re + s_im

    xr_ref[...], xi_ref[...] = lax.fori_loop(0, ct, scan_step, (xr_ref[...], xi_ref[...]), unroll=4)

    for gp in range(SP):
        def entering(e_ref):
            return jnp.concatenate([e_ref[pl.ds(gp * bsz + b, ct, stride=srows), :] for b in range(bsz)],
                                   axis=0)
        x_in = jnp.concatenate([entering(ere_ref), entering(eim_ref)], axis=1).astype(BF16)
        for hh in range(2):
            g = 2 * gp + hh
            yt_ref[g] = _dot(tt_ref[g], lt_ref[g]) + lax.dot_general(
                vt_ref[g], x_in, (((1,), (1,)), ((), ())), preferred_element_type=F32)

    for t in range(CHUNK):
        z = yt_ref[:, t * GROUP:(t + 1) * GROUP, :].reshape(SG * GROUP, cols).T
        z = z + d_ref[...] * u_refs[t][...].reshape(cols, CW).astype(F32)
        for b in range(bsz):
            for hf, sl in enumerate(halves):
                st_ref[b, hf, pl.ds(t, ct, stride=CHUNK), :] = z[b * ct:(b + 1) * ct, sl]
    for b in range(bsz):
        y_ref[b] = jnp.concatenate([st_ref[b, 0], st_ref[b, 1]], axis=1).astype(BF16)


def _ssm(proj3, tt_mat, w_mat, vt_mat, d_skip, a_re, a_im, bsz, seq, ct):
    tok = ct * CHUNK
    cols = bsz * ct
    kern = functools.partial(_ssm_kernel, bsz=bsz, ct=ct)
    ucol = TILE_U * SSM_SPLIT
    proj4 = proj3.reshape(bsz, seq // CHUNK, CHUNK * N_IN)
    u_specs = [pl.BlockSpec((bsz, ct, CW),
                            functools.partial(lambda s, q, t: (0, t, s * (N_IN // CW) + ucol + q), s))
               for s in range(CHUNK)]
    return pl.pallas_call(
        kern,
        grid=(SSM_SPLIT, seq // tok),
        in_specs=u_specs + [
            pl.BlockSpec((SG, CW, CW), lambda q, t: (q, 0, 0)),
            pl.BlockSpec((SP, 2 * CW, CW), lambda q, t: (q, 0, 0)),
            pl.BlockSpec((SG, CW, CW), lambda q, t: (q, 0, 0)),
            pl.BlockSpec((1, CW), lambda q, t: (0, q)),
            pl.BlockSpec((None, SP * bsz, 128), lambda q, t: (q, 0, 0)),
            pl.BlockSpec((None, SP * bsz, 128), lambda q, t: (q, 0, 0)),
        ],
        out_specs=pl.BlockSpec((bsz, tok, CW), lambda q, t: (0, t, q)),
        out_shape=jax.ShapeDtypeStruct((bsz, seq, D_SSM), BF16),
        scratch_shapes=[
            pltpu.VMEM((SG, CW, cols), BF16),
            pltpu.VMEM((SP * cols, 128), F32),
            pltpu.VMEM((SP * cols, 128), F32),
            pltpu.VMEM((SP * cols, 128), F32),
            pltpu.VMEM((SP * cols, 128), F32),
            pltpu.VMEM((SG, CW, cols), F32),
            pltpu.VMEM((bsz, 2, tok, 128), F32),
            pltpu.VMEM((SP * bsz, 128), F32),
            pltpu.VMEM((SP * bsz, 128), F32),
        ],
        compiler_params=pltpu.CompilerParams(
            dimension_semantics=("arbitrary", "arbitrary"), vmem_limit_bytes=VMEM_LIMIT),
        name="ssm_scan",
    )(*([proj4] * CHUNK), tt_mat, w_mat, vt_mat, d_skip, a_re, a_im)


def _chunk_decay_kernel(ldt_ref, lre_ref, lim_ref, are_ref, aim_ref):
    dt = jnp.exp(ldt_ref[...])
    mag = jnp.exp(float(CHUNK) * (lre_ref[...] * dt))
    ang = float(CHUNK) * (lim_ref[...] * dt)
    are_ref[...] = mag * jnp.cos(ang)
    aim_ref[...] = mag * jnp.sin(ang)


def _chunk_decay(ldt_p, lre_p, lim_p):
    shape = jax.ShapeDtypeStruct(ldt_p.shape, F32)
    return pl.pallas_call(_chunk_decay_kernel, out_shape=[shape, shape], name="ssm_chunk_decay")(
        ldt_p, lre_p, lim_p)


def _merge_kernel(x_ref, ya_ref, ys_ref, zs_ref, ga_ref, gs_ref, wg_ref, bg_ref,
                  wpa_ref, wps_ref, wo_ref, o_ref):
    yg = jax.nn.gelu(ys_ref[...].astype(F32))
    glu = _dot(yg.astype(BF16), wg_ref[...]) + bg_ref[...]
    y_ssm = (yg * jax.nn.sigmoid(glu) * zs_ref[...].astype(F32)).astype(BF16)
    merged = (ga_ref[...].astype(F32) * _dot(ya_ref[...], wpa_ref[...])
              + gs_ref[...].astype(F32) * _dot(y_ssm, wps_ref[...]))
    o_ref[...] = x_ref[...] + _dot(merged.astype(BF16), wo_ref[...])


def _merge(x2, y_att, yg, proj, w_glu, b_glu, w_pa, w_ps, w_out, tm):
    tokens = x2.shape[0]
    const = lambda i: (0, 0)
    once = pl.Buffered(1)
    return pl.pallas_call(
        _merge_kernel,
        grid=(tokens // tm,),
        in_specs=[
            pl.BlockSpec((tm, D_MODEL), lambda i: (i, 0)),
            pl.BlockSpec((tm, D_ATT), lambda i: (i, 0)),
            pl.BlockSpec((tm, D_SSM), lambda i: (i, 0)),
            pl.BlockSpec((tm, D_SSM), lambda i: (i, TILE_ZSSM)),
            pl.BlockSpec((tm, D_MODEL), lambda i: (i, TILE_GATT0 // 2)),
            pl.BlockSpec((tm, D_MODEL), lambda i: (i, TILE_GSSM0 // 2)),
            pl.BlockSpec((D_SSM, D_SSM), const, pipeline_mode=once),
            pl.BlockSpec((1, D_SSM), const, pipeline_mode=once),
            pl.BlockSpec((D_ATT, D_MODEL), const, pipeline_mode=once),
            pl.BlockSpec((D_SSM, D_MODEL), const, pipeline_mode=once),
            pl.BlockSpec((D_MODEL, D_MODEL), const, pipeline_mode=once),
        ],
        out_specs=pl.BlockSpec((tm, D_MODEL), lambda i: (i, 0)),
        out_shape=jax.ShapeDtypeStruct((tokens, D_MODEL), F32),
        compiler_params=pltpu.CompilerParams(
            dimension_semantics=("arbitrary",), vmem_limit_bytes=VMEM_LIMIT),
        name="merge_out",
    )(x2, y_att, yg, proj, proj, proj, w_glu, b_glu, w_pa, w_ps, w_out)


def kernel(x, ln_gain, w_in, q_norm_gain, k_norm_gain, lambda_q1, lambda_k1, lambda_q2, lambda_k2,
           subln_gain, ssm_lambda_re, ssm_lambda_im, ssm_log_dt, ssm_b_re, ssm_b_im, ssm_c_re,
           ssm_c_im, ssm_d, w_glu, b_glu, w_proj_att, w_proj_ssm, w_out):
    bsz, seq, _ = x.shape
    assert ln_gain.shape[0] == 1 and x.shape[2] == D_MODEL and w_in.shape[2] == N_IN
    tokens = bsz * seq
    tm_in = min(1024, seq)
    tq = min(1024, seq)
    tm_out = min(512, seq)
    ct = 128 // bsz
    assert 128 % bsz == 0 and ct % 8 == 0 and seq % (ct * CHUNK) == 0

    x2 = x.reshape(tokens, D_MODEL)
    scale = DQK ** -0.5
    q_gain = (jnp.tile(q_norm_gain[0], 2 * HEADS) * scale).reshape(1, 1024)
    k_gain = jnp.tile(k_norm_gain[0], 2 * HEADS).reshape(1, 1024)
    proj, vt = _in_proj(x2, ln_gain[0].reshape(1, D_MODEL), w_in[0].astype(BF16), q_gain, k_gain,
                        bsz, seq, tm_in)
    proj3 = proj.reshape(bsz, seq, N_IN)

    lam_params = jnp.stack([lambda_q1[0], lambda_k1[0], lambda_q2[0], lambda_k2[0]])
    subln = (subln_gain[0] * (1.0 - LAMBDA_INIT)).reshape(1, DV)
    score_bound = 1.02 * 8.0 * jnp.max(jnp.abs(q_norm_gain[0] * k_norm_gain[0]))
    attend = functools.partial(_attention, bsz=bsz, seq=seq, tq=tq)
    y_att = lax.cond(score_bound <= SCORE_BOUND_FAST,
                     functools.partial(attend, fast=True), functools.partial(attend, fast=False),
                     lam_params, subln, proj3, vt)

    tt_mat, w_mat, vt_mat = _ssm_params(ssm_log_dt[0], ssm_lambda_re[0], ssm_lambda_im[0],
                                        ssm_b_re[0], ssm_b_im[0], ssm_c_re[0], ssm_c_im[0])
    per_row = lambda a: jnp.repeat(a.reshape(PAIRS, 128), bsz, axis=0).reshape(SSM_SPLIT, SP * bsz, 128)
    a_re, a_im = _chunk_decay(per_row(jnp.repeat(ssm_log_dt[0], STATE)), per_row(ssm_lambda_re[0]),
                              per_row(ssm_lambda_im[0]))
    yg = _ssm(proj3, tt_mat, w_mat, vt_mat, ssm_d[0].reshape(1, D_SSM), a_re, a_im, bsz, seq, ct)

    out = _merge(x2, y_att.reshape(tokens, D_ATT), yg.reshape(tokens, D_SSM), proj, w_glu[0].astype(BF16),
                 b_glu[0].reshape(1, D_SSM), w_proj_att[0].astype(BF16), w_proj_ssm[0].astype(BF16),
                 w_out[0].astype(BF16), tm_out)
    return out.reshape(bsz, seq, D_MODEL)
```

```python
import functools
import math

import jax
import jax.numpy as jnp
from jax import lax
from jax.experimental import pallas as pl
from jax.experimental.pallas import tpu as pltpu

F32 = jnp.float32
BF16 = jnp.bfloat16
HIGHEST = lax.Precision.HIGHEST

D_MODEL = 2048
HEADS = 8
DQK = 64
DV = 2 * DQK
D_ATT = HEADS * DV
D_SSM = 1024
GROUP = 16
GROUPS = D_SSM // GROUP
PAIRS = GROUPS // 2
STATE = 64
N_IN = 6 * 1024 + 2 * D_MODEL
RMS_EPS = 1e-6
LAMBDA_INIT = 0.8 - 0.6 * math.exp(-0.3 * 0)
CHUNK = 16
CW = CHUNK * GROUP
SSM_SPLIT = 4
PARAM_BATCH = 4
SG = GROUPS // SSM_SPLIT
SP = SG // 2
NEG = -1e30
SCORE_BOUND_FAST = 30.0
KV_GROUP = 4
DIAG_SPLIT = 4

TILE_Q, TILE_K, TILE_V, TILE_ZATT, TILE_U, TILE_ZSSM = 0, 1, 2, 3, 4, 5
TILE_GATT0, TILE_GSSM0 = 6, 8
N_TILES = N_IN // 1024

VMEM_LIMIT = 56 * 1024 * 1024


def _dot(a, b):
    return jnp.dot(a, b, preferred_element_type=F32)


def _sigmoid(x):
    return 0.5 * jnp.tanh(0.5 * x) + 0.5


def _in_proj_kernel(x_ref, ln_ref, w_ref, qg_ref, kg_ref, gsum_ref, proj_ref, vt_ref, h_ref):
    j = pl.program_id(1)

    def group_rms_norm(acc, gain):
        sq = (acc * acc).astype(BF16)
        g = gsum_ref[...]
        ms = jnp.concatenate([_dot(sq[:, c * 256:(c + 1) * 256], g) for c in range(4)],
                             axis=1) * (1.0 / DQK)
        return acc * lax.rsqrt(ms + RMS_EPS) * gain

    @pl.when(j == TILE_Q)
    def _():
        x = x_ref[...]
        ms = jnp.mean(x * x, axis=-1, keepdims=True)
        h = (x * lax.rsqrt(ms + RMS_EPS) * ln_ref[...]).astype(BF16)
        h_ref[...] = h
        proj_ref[...] = group_rms_norm(_dot(h, w_ref[...]), qg_ref[...]).astype(BF16)

    @pl.when(j == TILE_K)
    def _():
        proj_ref[...] = group_rms_norm(_dot(h_ref[...], w_ref[...]), kg_ref[...]).astype(BF16)

    @pl.when(j == TILE_V)
    def _():
        acc = _dot(h_ref[...], w_ref[...])
        proj_ref[...] = acc.astype(BF16)
        vt_ref[...] = acc.T.astype(BF16)

    @pl.when(j == TILE_U)
    def _():
        proj_ref[...] = _dot(h_ref[...], w_ref[...]).astype(BF16)

    @pl.when((j == TILE_ZATT) | (j == TILE_ZSSM))
    def _():
        acc = _dot(h_ref[...], w_ref[...])
        proj_ref[...] = (acc * _sigmoid(acc)).astype(BF16)

    @pl.when(j >= TILE_GATT0)
    def _():
        proj_ref[...] = _sigmoid(_dot(h_ref[...], w_ref[...])).astype(BF16)


def _in_proj(x2, ln_gain, w_in, q_gain, k_gain, bsz, seq, tm):
    tokens = bsz * seq
    tiles_per_seq = seq // tm
    gi = lax.broadcasted_iota(jnp.int32, (256, 256), 0) // DQK
    gj = lax.broadcasted_iota(jnp.int32, (256, 256), 1) // DQK
    gsum = (gi == gj).astype(BF16)
    return pl.pallas_call(
        _in_proj_kernel,
        grid=(tokens // tm, N_TILES),
        in_specs=[
            pl.BlockSpec((tm, D_MODEL), lambda i, j: (i, 0)),
            pl.BlockSpec((1, D_MODEL), lambda i, j: (0, 0)),
            pl.BlockSpec((D_MODEL, 1024), lambda i, j: (0, j)),
            pl.BlockSpec((1, 1024), lambda i, j: (0, 0)),
            pl.BlockSpec((1, 1024), lambda i, j: (0, 0)),
            pl.BlockSpec((256, 256), lambda i, j: (0, 0)),
        ],
        out_specs=[
            pl.BlockSpec((tm, 1024), lambda i, j: (i, j)),
            pl.BlockSpec((None, D_ATT, tm), lambda i, j: (i // tiles_per_seq, 0, i % tiles_per_seq)),
        ],
        out_shape=[
            jax.ShapeDtypeStruct((tokens, N_IN), BF16),
            jax.ShapeDtypeStruct((bsz, D_ATT, seq), BF16),
        ],
        scratch_shapes=[pltpu.VMEM((tm, D_MODEL), BF16)],
        compiler_params=pltpu.CompilerParams(
            dimension_semantics=("arbitrary", "arbitrary"), vmem_limit_bytes=VMEM_LIMIT),
        name="in_proj",
    )(x2, ln_gain, w_in, q_gain, k_gain, gsum)


def _attn_kernel(lp_ref, sg_ref, q_ref, k_ref, vt_ref, z_ref, o_ref,
                 qbd_ref, m_ref, l_ref, acc_ref, *, seq, tq, fast):
    lp = lp_ref[...]
    lam = (jnp.exp(jnp.sum(lp[0:1] * lp[1:2], axis=-1, keepdims=True))
           - jnp.exp(jnp.sum(lp[2:3] * lp[3:4], axis=-1, keepdims=True)) + LAMBDA_INIT)

    w = tq // DIAG_SPLIT

    n_tiles = seq // tq

    def setup(i):
        qt = q_ref[pl.ds(pl.multiple_of(i * tq, tq), tq), :].astype(F32).T
        row = lax.broadcasted_iota(jnp.int32, (DV, w), 0)
        for c in range(DIAG_SPLIT):
            qc = qt[:, c * w:(c + 1) * w]
            qbd_ref[:, 2 * c * w:(2 * c + 1) * w] = jnp.where(row < DQK, qc, 0.0).astype(BF16)
            qbd_ref[:, (2 * c + 1) * w:(2 * c + 2) * w] = jnp.where(row >= DQK, qc, 0.0).astype(BF16)
        m_ref[...] = jnp.full(m_ref.shape, NEG, F32)
        l_ref[...] = jnp.zeros(l_ref.shape, F32)
        acc_ref[...] = jnp.zeros(acc_ref.shape, F32)

    def q_tile(i, carry):
        q0 = pl.multiple_of(i * tq, tq)

        def update(s, vt, state):
            m, l, acc = state
            if fast:
                p = jnp.exp(s)
                return m, l + jnp.sum(p, axis=0, keepdims=True), acc + _dot(vt, p.astype(BF16))
            m_new = jnp.maximum(m, jnp.max(s, axis=0, keepdims=True))
            alpha = jnp.exp(m - m_new)
            p = jnp.exp(s - m_new)
            return (m_new, alpha * l + jnp.sum(p, axis=0, keepdims=True),
                    alpha * acc + _dot(vt, p.astype(BF16)))

        def block(j, state):
            k0 = pl.multiple_of(j * tq, tq)
            s = _dot(k_ref[pl.ds(k0, tq), :], qbd_ref[...])
            return update(s, vt_ref[:, pl.ds(k0, tq)], state)

        def diagonal(state):
            kpos = lax.broadcasted_iota(jnp.int32, (w, 2 * w), 0)
            qpos = lax.broadcasted_iota(jnp.int32, (w, 2 * w), 1)
            causal = kpos <= jnp.where(qpos >= w, qpos - w, qpos)
            for d in range(DIAG_SPLIT):
                c0 = d * 2 * w
                k0 = pl.multiple_of(q0 + d * w, w)
                s = _dot(k_ref[pl.ds(k0, w), :], qbd_ref[:, c0:])
                tri = jnp.where(causal, s[:, :2 * w], NEG)
                s = tri if d == DIAG_SPLIT - 1 else jnp.concatenate([tri, s[:, 2 * w:]], axis=1)
                new = update(s, vt_ref[:, pl.ds(k0, w)], tuple(a[:, c0:] for a in state))
                state = tuple(n if c0 == 0 else jnp.concatenate([a[:, :c0], n], axis=1)
                              for a, n in zip(state, new))
            return state

        def load_state():
            return m_ref[...], l_ref[...], acc_ref[...]

        def group(jj, c):
            state = load_state()
            for u in range(KV_GROUP):
                state = block(jj * KV_GROUP + u, state)
            m_ref[...], l_ref[...], acc_ref[...] = state
            return c

        n_groups = lax.shift_right_logical(i, KV_GROUP.bit_length() - 1)
        lax.fori_loop(0, n_groups, group, 0)

        for rem in range(KV_GROUP):
            @pl.when((i & (KV_GROUP - 1)) == rem)
            def _():
                state = load_state()
                for u in range(rem):
                    state = block(n_groups * KV_GROUP + u, state)
                _, l, acc = diagonal(state)
                o = acc * (1.0 / l)
                a = jnp.concatenate(
                    [o[:, 2 * c * w:(2 * c + 1) * w] - lam * o[:, (2 * c + 1) * w:(2 * c + 2) * w]
                     for c in range(DIAG_SPLIT)], axis=1)
                ms = jnp.mean(a * a, axis=0, keepdims=True)
                n = (a * lax.rsqrt(ms + RMS_EPS)).T
                out = n * sg_ref[...] * z_ref[pl.ds(q0, tq), :].astype(F32)
                o_ref[pl.ds(q0, tq), :] = out.astype(BF16)
                setup(jnp.minimum(i + 1, n_tiles - 1))
        return carry

    setup(0)
    lax.fori_loop(0, n_tiles, q_tile, 0)


def _attention(lam_params, subln, proj3, vt, *, bsz, seq, tq, fast):
    kern = functools.partial(_attn_kernel, seq=seq, tq=tq, fast=fast)
    qcol, kcol, zcol = TILE_Q * HEADS, TILE_K * HEADS, TILE_ZATT * HEADS
    return pl.pallas_call(
        kern,
        grid=(bsz, HEADS),
        in_specs=[
            pl.BlockSpec((4, DQK), lambda b, h: (0, 0)),
            pl.BlockSpec((1, DV), lambda b, h: (0, 0)),
            pl.BlockSpec((None, seq, DV), lambda b, h: (b, 0, qcol + h)),
            pl.BlockSpec((None, seq, DV), lambda b, h: (b, 0, kcol + h)),
            pl.BlockSpec((None, DV, seq), lambda b, h: (b, h, 0)),
            pl.BlockSpec((None, seq, DV), lambda b, h: (b, 0, zcol + h)),
        ],
        out_specs=pl.BlockSpec((None, seq, DV), lambda b, h: (b, 0, h)),
        out_shape=jax.ShapeDtypeStruct((bsz, seq, D_ATT), BF16),
        scratch_shapes=[
            pltpu.VMEM((DV, 2 * tq), BF16),
            pltpu.VMEM((1, 2 * tq), F32),
            pltpu.VMEM((1, 2 * tq), F32),
            pltpu.VMEM((DV, 2 * tq), F32),
        ],
        compiler_params=pltpu.CompilerParams(
            dimension_semantics=("arbitrary", "arbitrary"), vmem_limit_bytes=VMEM_LIMIT),
        name="diff_attention",
    )(lam_params, subln, proj3, proj3, vt, proj3)


def _ssm_param_kernel(ldt_ref, lre_ref, lim_ref, lrec_ref, limc_ref, btre_ref, btim_ref,
                      c4re_ref, c4im_ref, bpre_ref, bpim_ref, cre_ref, cim_ref,
                      tt_ref, w_ref, vt_ref):
    for gi in range(PARAM_BATCH):
        _ssm_param_group(
            gi % 2, ldt_ref.at[gi], lre_ref.at[gi], lim_ref.at[gi], lrec_ref.at[gi], limc_ref.at[gi],
            btre_ref.at[gi], btim_ref.at[gi], c4re_ref.at[gi], c4im_ref.at[gi], bpre_ref.at[gi],
            bpim_ref.at[gi], cre_ref.at[gi], cim_ref.at[gi],
            tt_ref.at[gi], w_ref.at[gi // 2, pl.ds((gi % 2) * CW, CW)], vt_ref.at[gi])


def _ssm_param_group(h, ldt_ref, lre_ref, lim_ref, lrec_ref, limc_ref, btre_ref, btim_ref,
                     c4re_ref, c4im_ref, bpre_ref, bpim_ref, cre_ref, cim_ref, tt_ref, w_ref, vt_ref):
    dt = jnp.exp(ldt_ref[...])

    def cis_pow(n, lr_, li_):
        mag = jnp.exp(n * (lr_ * dt))
        ang = n * (li_ * dt)
        return mag * jnp.cos(ang), mag * jnp.sin(ang)

    def zoh(a_re, a_im, lr_, li_):
        nr, ni = a_re - 1.0, a_im
        den = lr_ * lr_ + li_ * li_
        return (nr * lr_ + ni * li_) / den, (ni * lr_ - nr * li_) / den

    def complex_mul(x_re, x_im, y_re, y_im):
        return x_re * y_re - x_im * y_im, x_re * y_im + x_im * y_re

    lr, li = lre_ref[...], lim_ref[...]
    n_rows = jnp.minimum(lax.broadcasted_iota(jnp.int32, (24, CW), 0), CHUNK).astype(F32)
    p_re, p_im = cis_pow(n_rows, lr, li)
    coef_re, coef_im = zoh(p_re[1:2], p_im[1:2], lr, li)
    bb_re, bb_im = complex_mul(coef_re, coef_im, btre_ref[...], btim_ref[...])
    c4_re, c4_im = c4re_ref[...], c4im_ref[...]
    slot = lax.broadcasted_iota(jnp.int32, (GROUP, CW), 1) // STATE

    def to_slots(v_re, v_im):
        return jnp.where(slot == h, v_re, jnp.where(slot == h + 2, v_im, 0.0)).astype(BF16)

    for s in range(CHUNK):
        n = CHUNK - 1 - s
        w_re, w_im = complex_mul(bb_re, bb_im, p_re[n:n + 1], p_im[n:n + 1])
        w_ref[s * GROUP:(s + 1) * GROUP, :] = to_slots(w_re, w_im)
        v_re, v_im = complex_mul(c4_re, c4_im, p_re[s + 1:s + 2], p_im[s + 1:s + 2])
        vt_ref[s * GROUP:(s + 1) * GROUP, :] = to_slots(v_re, -v_im)

    lrc, lic = lrec_ref[...], limc_ref[...]
    n_lanes = jnp.minimum(lax.broadcasted_iota(jnp.int32, (STATE, 128), 1), CHUNK).astype(F32)
    q_re, q_im = cis_pow(n_lanes, lrc, lic)
    cc_re, cc_im = zoh(q_re[:, 1:2], q_im[:, 1:2], lrc, lic)
    bp_re, bp_im = complex_mul(cc_re, cc_im, bpre_ref[...], bpim_ref[...])
    sel = lax.broadcasted_iota(jnp.int32, (128, CW), 0)
    lane_r = lax.broadcasted_iota(jnp.int32, (128, CW), 1) // GROUP
    lane_ci = lax.broadcasted_iota(jnp.int32, (128, CW), 1) % GROUP
    e_pow = (sel == CHUNK - 1 - lane_r).astype(F32)
    e_ci = (sel == lane_ci).astype(F32)

    def hdot(a, b):
        return jnp.dot(a, b, preferred_element_type=F32, precision=HIGHEST)

    r_re, r_im = complex_mul(hdot(q_re, e_pow), hdot(q_im, e_pow), hdot(bp_re, e_ci), hdot(bp_im, e_ci))
    krev = hdot(cre_ref[...], r_re) - hdot(cim_ref[...], r_im)
    lane_s = lax.broadcasted_iota(jnp.int32, (GROUP, CW), 1) // GROUP
    for t in range(CHUNK):
        shift = (CHUNK - 1 - t) * GROUP
        moved = krev if shift == 0 else pltpu.roll(krev, CW - shift, axis=1)
        tt_ref[t * GROUP:(t + 1) * GROUP, :] = jnp.where(lane_s <= t, moved, 0.0).astype(BF16)


def _ssm_params(log_dt, lam_re, lam_im, b_re, b_im, c_re, c_im):
    tile4 = lambda a: jnp.tile(a, (1, 1, 4))
    pad128 = lambda a: jnp.pad(a, ((0, 0), (0, 0), (0, 128 - GROUP)))
    ldt = log_dt.reshape(GROUPS, 1, 1)
    lre4 = tile4(lam_re.reshape(GROUPS, 1, STATE))
    lim4 = tile4(lam_im.reshape(GROUPS, 1, STATE))
    lrec = lam_re.reshape(GROUPS, STATE, 1)
    limc = lam_im.reshape(GROUPS, STATE, 1)
    bt_re = tile4(jnp.swapaxes(b_re, 1, 2))
    bt_im = tile4(jnp.swapaxes(b_im, 1, 2))
    c4_re, c4_im = tile4(c_re), tile4(c_im)
    bp_re, bp_im = pad128(b_re), pad128(b_im)

    def spec(shape):
        return pl.BlockSpec((PARAM_BATCH,) + shape, lambda g: (g, 0, 0))

    return pl.pallas_call(
        _ssm_param_kernel,
        grid=(GROUPS // PARAM_BATCH,),
        in_specs=[spec((1, 1)), spec((1, CW)), spec((1, CW)), spec((STATE, 1)), spec((STATE, 1)),
                  spec((GROUP, CW)), spec((GROUP, CW)), spec((GROUP, CW)), spec((GROUP, CW)),
                  spec((STATE, 128)), spec((STATE, 128)), spec((GROUP, STATE)), spec((GROUP, STATE))],
        out_specs=[
            pl.BlockSpec((PARAM_BATCH, CW, CW), lambda g: (g, 0, 0)),
            pl.BlockSpec((PARAM_BATCH // 2, 2 * CW, CW), lambda g: (g, 0, 0)),
            pl.BlockSpec((PARAM_BATCH, CW, CW), lambda g: (g, 0, 0)),
        ],
        out_shape=[
            jax.ShapeDtypeStruct((GROUPS, CW, CW), BF16),
            jax.ShapeDtypeStruct((PAIRS, 2 * CW, CW), BF16),
            jax.ShapeDtypeStruct((GROUPS, CW, CW), BF16),
        ],
        compiler_params=pltpu.CompilerParams(dimension_semantics=("arbitrary",)),
        name="ssm_params",
    )(ldt, lre4, lim4, lrec, limc, bt_re, bt_im, c4_re, c4_im, bp_re, bp_im, c_re, c_im)


def _ssm_kernel(u_ref, tt_ref, w_ref, vt_ref, d_ref, are_ref, aim_ref, y_ref,
                uf_ref, lt_ref, sre_ref, sim_ref, yt_ref, st_ref, xr_ref, xi_ref, *, bsz, ct):
    cols = bsz * ct
    halves = [slice(0, 128), slice(128, 256)]

    @pl.when(pl.program_id(1) == 0)
    def _():
        xr_ref[...] = jnp.zeros(xr_ref.shape, F32)
        xi_ref[...] = jnp.zeros(xi_ref.shape, F32)

    for b in range(bsz):
        u = u_ref[b].astype(F32)
        for hf, sl in enumerate(halves):
            uf_ref[b, hf] = u[:, sl]
    for s in range(CHUNK):
        x_s = jnp.concatenate(
            [jnp.concatenate([uf_ref[b, hf, pl.ds(s, ct, stride=CHUNK), :] for b in range(bsz)], axis=0)
             for hf in range(2)], axis=1)
        lt_ref[:, s * GROUP:(s + 1) * GROUP, :] = x_s.T.astype(BF16).reshape(SG, GROUP, cols)

    def pair_rows(gp):
        return slice(gp * cols, (gp + 1) * cols)

    for gp in range(SP):
        lt_pair = lt_ref[2 * gp:2 * gp + 2].reshape(2 * CW, cols)
        s = lax.dot_general(lt_pair, w_ref[gp], (((0,), (0,)), ((), ())), preferred_element_type=F32)
        sre_ref[pair_rows(gp), :] = s[:, :128]
        sim_ref[pair_rows(gp), :] = s[:, 128:]

    a_re, a_im = are_ref[...], aim_ref[...]

    def scan_step(n, c):
        rows = pl.ds(n, SP * bsz, stride=ct)
        s_re, s_im = sre_ref[rows, :], sim_ref[rows, :]
        x_re, x_im = xr_ref[...], xi_ref[...]
        sre_ref[rows, :] = x_re
        sim_ref[rows, :] = x_im
        xr_ref[...] = a_re * x_re - a_im * x_im + s_re
        xi_ref[...] = a_re * x_im + a_im * x_re + s_im
        return c

    lax.fori_loop(0, ct, scan_step, 0, unroll=4)

    for gp in range(SP):
        x_in = jnp.concatenate([sre_ref[pair_rows(gp), :], sim_ref[pair_rows(gp), :]],
                               axis=1).astype(BF16)
        for hh in range(2):
            g = 2 * gp + hh
            yt_ref[g] = _dot(tt_ref[g], lt_ref[g]) + lax.dot_general(
                vt_ref[g], x_in, (((1,), (1,)), ((), ())), preferred_element_type=F32)

    for t in range(CHUNK):
        z = yt_ref[:, t * GROUP:(t + 1) * GROUP, :].reshape(SG * GROUP, cols).T
        for b in range(bsz):
            for hf, sl in enumerate(halves):
                st_ref[b, hf, pl.ds(t, ct, stride=CHUNK), :] = z[b * ct:(b + 1) * ct, sl]
    for b in range(bsz):
        pre = jnp.concatenate([st_ref[b, 0], st_ref[b, 1]], axis=1)
        u = jnp.concatenate([uf_ref[b, 0], uf_ref[b, 1]], axis=1)
        y_ref[b] = jax.nn.gelu(pre + d_ref[...] * u).astype(BF16)


def _ssm(proj3, tt_mat, w_mat, vt_mat, d_skip, a_re, a_im, bsz, seq, ct):
    tok = ct * CHUNK
    cols = bsz * ct
    kern = functools.partial(_ssm_kernel, bsz=bsz, ct=ct)
    ucol = TILE_U * SSM_SPLIT
    return pl.pallas_call(
        kern,
        grid=(SSM_SPLIT, seq // tok),
        in_specs=[
            pl.BlockSpec((bsz, tok, CW), lambda q, t: (0, t, ucol + q)),
            pl.BlockSpec((SG, CW, CW), lambda q, t: (q, 0, 0)),
            pl.BlockSpec((SP, 2 * CW, CW), lambda q, t: (q, 0, 0)),
            pl.BlockSpec((SG, CW, CW), lambda q, t: (q, 0, 0)),
            pl.BlockSpec((1, CW), lambda q, t: (0, q)),
            pl.BlockSpec((None, SP * bsz, 128), lambda q, t: (q, 0, 0)),
            pl.BlockSpec((None, SP * bsz, 128), lambda q, t: (q, 0, 0)),
        ],
        out_specs=pl.BlockSpec((bsz, tok, CW), lambda q, t: (0, t, q)),
        out_shape=jax.ShapeDtypeStruct((bsz, seq, D_SSM), BF16),
        scratch_shapes=[
            pltpu.VMEM((bsz, 2, tok, 128), F32),
            pltpu.VMEM((SG, CW, cols), BF16),
            pltpu.VMEM((SP * cols, 128), F32),
            pltpu.VMEM((SP * cols, 128), F32),
            pltpu.VMEM((SG, CW, cols), F32),
            pltpu.VMEM((bsz, 2, tok, 128), F32),
            pltpu.VMEM((SP * bsz, 128), F32),
            pltpu.VMEM((SP * bsz, 128), F32),
        ],
        compiler_params=pltpu.CompilerParams(
            dimension_semantics=("arbitrary", "arbitrary"), vmem_limit_bytes=VMEM_LIMIT),
        name="ssm_scan",
    )(proj3, tt_mat, w_mat, vt_mat, d_skip, a_re, a_im)


def _chunk_decay_kernel(ldt_ref, lre_ref, lim_ref, are_ref, aim_ref):
    dt = jnp.exp(ldt_ref[...])
    mag = jnp.exp(float(CHUNK) * (lre_ref[...] * dt))
    ang = float(CHUNK) * (lim_ref[...] * dt)
    are_ref[...] = mag * jnp.cos(ang)
    aim_ref[...] = mag * jnp.sin(ang)


def _chunk_decay(ldt_p, lre_p, lim_p):
    shape = jax.ShapeDtypeStruct(ldt_p.shape, F32)
    return pl.pallas_call(_chunk_decay_kernel, out_shape=[shape, shape], name="ssm_chunk_decay")(
        ldt_p, lre_p, lim_p)


def _merge_kernel(x_ref, ya_ref, yg_ref, zs_ref, ga_ref, gs_ref, wg_ref, bg_ref,
                  wpa_ref, wps_ref, wo_ref, o_ref):
    yg = yg_ref[...]
    glu = _dot(yg, wg_ref[...]) + bg_ref[...]
    y_ssm = (yg.astype(F32) * jax.nn.sigmoid(glu) * zs_ref[...].astype(F32)).astype(BF16)
    merged = (ga_ref[...].astype(F32) * _dot(ya_ref[...], wpa_ref[...])
              + gs_ref[...].astype(F32) * _dot(y_ssm, wps_ref[...]))
    o_ref[...] = x_ref[...] + _dot(merged.astype(BF16), wo_ref[...])


def _merge(x2, y_att, yg, proj, w_glu, b_glu, w_pa, w_ps, w_out, tm):
    tokens = x2.shape[0]
    const = lambda i: (0, 0)
    once = pl.Buffered(1)
    return pl.pallas_call(
        _merge_kernel,
        grid=(tokens // tm,),
        in_specs=[
            pl.BlockSpec((tm, D_MODEL), lambda i: (i, 0)),
            pl.BlockSpec((tm, D_ATT), lambda i: (i, 0)),
            pl.BlockSpec((tm, D_SSM), lambda i: (i, 0)),
            pl.BlockSpec((tm, D_SSM), lambda i: (i, TILE_ZSSM)),
            pl.BlockSpec((tm, D_MODEL), lambda i: (i, TILE_GATT0 // 2)),
            pl.BlockSpec((tm, D_MODEL), lambda i: (i, TILE_GSSM0 // 2)),
            pl.BlockSpec((D_SSM, D_SSM), const, pipeline_mode=once),
            pl.BlockSpec((1, D_SSM), const, pipeline_mode=once),
            pl.BlockSpec((D_ATT, D_MODEL), const, pipeline_mode=once),
            pl.BlockSpec((D_SSM, D_MODEL), const, pipeline_mode=once),
            pl.BlockSpec((D_MODEL, D_MODEL), const, pipeline_mode=once),
        ],
        out_specs=pl.BlockSpec((tm, D_MODEL), lambda i: (i, 0)),
        out_shape=jax.ShapeDtypeStruct((tokens, D_MODEL), F32),
        compiler_params=pltpu.CompilerParams(
            dimension_semantics=("arbitrary",), vmem_limit_bytes=VMEM_LIMIT),
        name="merge_out",
    )(x2, y_att, yg, proj, proj, proj, w_glu, b_glu, w_pa, w_ps, w_out)


def kernel(x, ln_gain, w_in, q_norm_gain, k_norm_gain, lambda_q1, lambda_k1, lambda_q2, lambda_k2,
           subln_gain, ssm_lambda_re, ssm_lambda_im, ssm_log_dt, ssm_b_re, ssm_b_im, ssm_c_re,
           ssm_c_im, ssm_d, w_glu, b_glu, w_proj_att, w_proj_ssm, w_out):
    bsz, seq, _ = x.shape
    assert ln_gain.shape[0] == 1 and x.shape[2] == D_MODEL and w_in.shape[2] == N_IN
    tokens = bsz * seq
    tm_in = min(1024, seq)
    tq = min(1024, seq)
    tm_out = min(512, seq)
    ct = 128 // bsz
    assert 128 % bsz == 0 and ct % 8 == 0 and seq % (ct * CHUNK) == 0

    x2 = x.reshape(tokens, D_MODEL)
    scale = DQK ** -0.5
    q_gain = (jnp.tile(q_norm_gain[0], 2 * HEADS) * scale).reshape(1, 1024)
    k_gain = jnp.tile(k_norm_gain[0], 2 * HEADS).reshape(1, 1024)
    proj, vt = _in_proj(x2, ln_gain[0].reshape(1, D_MODEL), w_in[0].astype(BF16), q_gain, k_gain,
                        bsz, seq, tm_in)
    proj3 = proj.reshape(bsz, seq, N_IN)

    lam_params = jnp.stack([lambda_q1[0], lambda_k1[0], lambda_q2[0], lambda_k2[0]])
    subln = (subln_gain[0] * (1.0 - LAMBDA_INIT)).reshape(1, DV)
    score_bound = 1.02 * 8.0 * jnp.max(jnp.abs(q_norm_gain[0] * k_norm_gain[0]))
    attend = functools.partial(_attention, bsz=bsz, seq=seq, tq=tq)
    y_att = lax.cond(score_bound <= SCORE_BOUND_FAST,
                     functools.partial(attend, fast=True), functools.partial(attend, fast=False),
                     lam_params, subln, proj3, vt)

    tt_mat, w_mat, vt_mat = _ssm_params(ssm_log_dt[0], ssm_lambda_re[0], ssm_lambda_im[0],
                                        ssm_b_re[0], ssm_b_im[0], ssm_c_re[0], ssm_c_im[0])
    per_row = lambda a: jnp.repeat(a.reshape(PAIRS, 128), bsz, axis=0).reshape(SSM_SPLIT, SP * bsz, 128)
    a_re, a_im = _chunk_decay(per_row(jnp.repeat(ssm_log_dt[0], STATE)), per_row(ssm_lambda_re[0]),
                              per_row(ssm_lambda_im[0]))
    yg = _ssm(proj3, tt_mat, w_mat, vt_mat, ssm_d[0].reshape(1, D_SSM), a_re, a_im, bsz, seq, ct)

    out = _merge(x2, y_att.reshape(tokens, D_ATT), yg.reshape(tokens, D_SSM), proj, w_glu[0].astype(BF16),
                 b_glu[0].reshape(1, D_SSM), w_proj_att[0].astype(BF16), w_proj_ssm[0].astype(BF16),
                 w_out[0].astype(BF16), tm_out)
    return out.reshape(bsz, seq, D_MODEL)
```

```python
import functools
import math

import jax
import jax.numpy as jnp
from jax import lax
from jax.experimental import pallas as pl
from jax.experimental.pallas import tpu as pltpu

F32 = jnp.float32
BF16 = jnp.bfloat16
HIGHEST = lax.Precision.HIGHEST

D_MODEL = 2048
HEADS = 8
DQK = 64
DV = 2 * DQK
D_ATT = HEADS * DV
D_SSM = 1024
GROUP = 16
GROUPS = D_SSM // GROUP
PAIRS = GROUPS // 2
STATE = 64
N_IN = 6 * 1024 + 2 * D_MODEL
RMS_EPS = 1e-6
LAMBDA_INIT = 0.8 - 0.6 * math.exp(-0.3 * 0)
CHUNK = 16
CW = CHUNK * GROUP
SSM_SPLIT = 4
PARAM_BATCH = 4
SG = GROUPS // SSM_SPLIT
SP = SG // 2
NEG = -1e30
SCORE_BOUND_FAST = 0.0
KV_GROUP = 4
DIAG_SPLIT = 4

TILE_Q, TILE_K, TILE_V, TILE_ZATT, TILE_U, TILE_ZSSM = 0, 1, 2, 3, 4, 5
TILE_GATT0, TILE_GSSM0 = 6, 8
N_TILES = N_IN // 1024

VMEM_LIMIT = 56 * 1024 * 1024


def _dot(a, b):
    return jnp.dot(a, b, preferred_element_type=F32)


def _sigmoid(x):
    return 0.5 * jnp.tanh(0.5 * x) + 0.5


def _in_proj_kernel(x_ref, ln_ref, w_ref, qg_ref, kg_ref, gsum_ref, proj_ref, vt_ref, h_ref):
    j = pl.program_id(1)

    def group_rms_norm(acc, gain):
        sq = (acc * acc).astype(BF16)
        g = gsum_ref[...]
        ms = jnp.concatenate([_dot(sq[:, c * 256:(c + 1) * 256], g) for c in range(4)],
                             axis=1) * (1.0 / DQK)
        return acc * lax.rsqrt(ms + RMS_EPS) * gain

    @pl.when(j == TILE_Q)
    def _():
        x = x_ref[...]
        ms = jnp.mean(x * x, axis=-1, keepdims=True)
        h = (x * lax.rsqrt(ms + RMS_EPS) * ln_ref[...]).astype(BF16)
        h_ref[...] = h
        proj_ref[...] = group_rms_norm(_dot(h, w_ref[...]), qg_ref[...]).astype(BF16)

    @pl.when(j == TILE_K)
    def _():
        proj_ref[...] = group_rms_norm(_dot(h_ref[...], w_ref[...]), kg_ref[...]).astype(BF16)

    @pl.when(j == TILE_V)
    def _():
        acc = _dot(h_ref[...], w_ref[...])
        proj_ref[...] = acc.astype(BF16)
        vt_ref[...] = acc.T.astype(BF16)

    @pl.when(j == TILE_U)
    def _():
        proj_ref[...] = _dot(h_ref[...], w_ref[...]).astype(BF16)

    @pl.when((j == TILE_ZATT) | (j == TILE_ZSSM))
    def _():
        acc = _dot(h_ref[...], w_ref[...])
        proj_ref[...] = (acc * _sigmoid(acc)).astype(BF16)

    @pl.when(j >= TILE_GATT0)
    def _():
        proj_ref[...] = _sigmoid(_dot(h_ref[...], w_ref[...])).astype(BF16)


def _in_proj(x2, ln_gain, w_in, q_gain, k_gain, bsz, seq, tm):
    tokens = bsz * seq
    tiles_per_seq = seq // tm
    gi = lax.broadcasted_iota(jnp.int32, (256, 256), 0) // DQK
    gj = lax.broadcasted_iota(jnp.int32, (256, 256), 1) // DQK
    gsum = (gi == gj).astype(BF16)
    return pl.pallas_call(
        _in_proj_kernel,
        grid=(tokens // tm, N_TILES),
        in_specs=[
            pl.BlockSpec((tm, D_MODEL), lambda i, j: (i, 0)),
            pl.BlockSpec((1, D_MODEL), lambda i, j: (0, 0)),
            pl.BlockSpec((D_MODEL, 1024), lambda i, j: (0, j)),
            pl.BlockSpec((1, 1024), lambda i, j: (0, 0)),
            pl.BlockSpec((1, 1024), lambda i, j: (0, 0)),
            pl.BlockSpec((256, 256), lambda i, j: (0, 0)),
        ],
        out_specs=[
            pl.BlockSpec((tm, 1024), lambda i, j: (i, j)),
            pl.BlockSpec((None, D_ATT, tm), lambda i, j: (i // tiles_per_seq, 0, i % tiles_per_seq)),
        ],
        out_shape=[
            jax.ShapeDtypeStruct((tokens, N_IN), BF16),
            jax.ShapeDtypeStruct((bsz, D_ATT, seq), BF16),
        ],
        scratch_shapes=[pltpu.VMEM((tm, D_MODEL), BF16)],
        compiler_params=pltpu.CompilerParams(
            dimension_semantics=("arbitrary", "arbitrary"), vmem_limit_bytes=VMEM_LIMIT),
        name="in_proj",
    )(x2, ln_gain, w_in, q_gain, k_gain, gsum)


def _attn_kernel(lp_ref, sg_ref, q_ref, k_ref, vt_ref, z_ref, o_ref,
                 qbd_ref, m_ref, l_ref, acc_ref, *, seq, tq, fast):
    lp = lp_ref[...]
    lam = (jnp.exp(jnp.sum(lp[0:1] * lp[1:2], axis=-1, keepdims=True))
           - jnp.exp(jnp.sum(lp[2:3] * lp[3:4], axis=-1, keepdims=True)) + LAMBDA_INIT)

    w = tq // DIAG_SPLIT

    n_tiles = seq // tq

    def setup(i):
        qt = q_ref[pl.ds(pl.multiple_of(i * tq, tq), tq), :].astype(F32).T
        row = lax.broadcasted_iota(jnp.int32, (DV, w), 0)
        for c in range(DIAG_SPLIT):
            qc = qt[:, c * w:(c + 1) * w]
            qbd_ref[:, 2 * c * w:(2 * c + 1) * w] = jnp.where(row < DQK, qc, 0.0).astype(BF16)
            qbd_ref[:, (2 * c + 1) * w:(2 * c + 2) * w] = jnp.where(row >= DQK, qc, 0.0).astype(BF16)
        m_ref[...] = jnp.full(m_ref.shape, NEG, F32)
        l_ref[...] = jnp.zeros(l_ref.shape, F32)
        acc_ref[...] = jnp.zeros(acc_ref.shape, F32)

    def q_tile(i, carry):
        q0 = pl.multiple_of(i * tq, tq)

        def update(s, vt, state):
            m, l, acc = state
            if fast:
                p = jnp.exp(s)
                return m, l + jnp.sum(p, axis=0, keepdims=True), acc + _dot(vt, p.astype(BF16))
            m_new = jnp.maximum(m, jnp.max(s, axis=0, keepdims=True))
            alpha = jnp.exp(m - m_new)
            p = jnp.exp(s - m_new)
            return (m_new, alpha * l + jnp.sum(p, axis=0, keepdims=True),
                    alpha * acc + _dot(vt, p.astype(BF16)))

        def block(j, state):
            k0 = pl.multiple_of(j * tq, tq)
            s = _dot(k_ref[pl.ds(k0, tq), :], qbd_ref[...])
            return update(s, vt_ref[:, pl.ds(k0, tq)], state)

        def diagonal(state):
            kpos = lax.broadcasted_iota(jnp.int32, (w, 2 * w), 0)
            qpos = lax.broadcasted_iota(jnp.int32, (w, 2 * w), 1)
            causal = kpos <= jnp.where(qpos >= w, qpos - w, qpos)
            for d in range(DIAG_SPLIT):
                c0 = d * 2 * w
                k0 = pl.multiple_of(q0 + d * w, w)
                s = _dot(k_ref[pl.ds(k0, w), :], qbd_ref[:, c0:])
                tri = jnp.where(causal, s[:, :2 * w], NEG)
                s = tri if d == DIAG_SPLIT - 1 else jnp.concatenate([tri, s[:, 2 * w:]], axis=1)
                new = update(s, vt_ref[:, pl.ds(k0, w)], tuple(a[:, c0:] for a in state))
                state = tuple(n if c0 == 0 else jnp.concatenate([a[:, :c0], n], axis=1)
                              for a, n in zip(state, new))
            return state

        def load_state():
            return m_ref[...], l_ref[...], acc_ref[...]

        def group(jj, c):
            state = load_state()
            for u in range(KV_GROUP):
                state = block(jj * KV_GROUP + u, state)
            m_ref[...], l_ref[...], acc_ref[...] = state
            return c

        n_groups = lax.shift_right_logical(i, KV_GROUP.bit_length() - 1)
        lax.fori_loop(0, n_groups, group, 0)

        for rem in range(KV_GROUP):
            @pl.when((i & (KV_GROUP - 1)) == rem)
            def _():
                state = load_state()
                for u in range(rem):
                    state = block(n_groups * KV_GROUP + u, state)
                _, l, acc = diagonal(state)
                o = acc * (1.0 / l)
                a = jnp.concatenate(
                    [o[:, 2 * c * w:(2 * c + 1) * w] - lam * o[:, (2 * c + 1) * w:(2 * c + 2) * w]
                     for c in range(DIAG_SPLIT)], axis=1)
                ms = jnp.mean(a * a, axis=0, keepdims=True)
                n = (a * lax.rsqrt(ms + RMS_EPS)).T
                out = n * sg_ref[...] * z_ref[pl.ds(q0, tq), :].astype(F32)
                o_ref[pl.ds(q0, tq), :] = out.astype(BF16)
                setup(jnp.minimum(i + 1, n_tiles - 1))
        return carry

    setup(0)
    lax.fori_loop(0, n_tiles, q_tile, 0)


def _attention(lam_params, subln, proj3, vt, *, bsz, seq, tq, fast):
    kern = functools.partial(_attn_kernel, seq=seq, tq=tq, fast=fast)
    qcol, kcol, zcol = TILE_Q * HEADS, TILE_K * HEADS, TILE_ZATT * HEADS
    return pl.pallas_call(
        kern,
        grid=(bsz, HEADS),
        in_specs=[
            pl.BlockSpec((4, DQK), lambda b, h: (0, 0)),
            pl.BlockSpec((1, DV), lambda b, h: (0, 0)),
            pl.BlockSpec((None, seq, DV), lambda b, h: (b, 0, qcol + h)),
            pl.BlockSpec((None, seq, DV), lambda b, h: (b, 0, kcol + h)),
            pl.BlockSpec((None, DV, seq), lambda b, h: (b, h, 0)),
            pl.BlockSpec((None, seq, DV), lambda b, h: (b, 0, zcol + h)),
        ],
        out_specs=pl.BlockSpec((None, seq, DV), lambda b, h: (b, 0, h)),
        out_shape=jax.ShapeDtypeStruct((bsz, seq, D_ATT), BF16),
        scratch_shapes=[
            pltpu.VMEM((DV, 2 * tq), BF16),
            pltpu.VMEM((1, 2 * tq), F32),
            pltpu.VMEM((1, 2 * tq), F32),
            pltpu.VMEM((DV, 2 * tq), F32),
        ],
        compiler_params=pltpu.CompilerParams(
            dimension_semantics=("arbitrary", "arbitrary"), vmem_limit_bytes=VMEM_LIMIT),
        name="diff_attention",
    )(lam_params, subln, proj3, proj3, vt, proj3)


def _ssm_param_kernel(ldt_ref, lre_ref, lim_ref, lrec_ref, limc_ref, btre_ref, btim_ref,
                      c4re_ref, c4im_ref, bpre_ref, bpim_ref, cre_ref, cim_ref,
                      tt_ref, w_ref, vt_ref):
    for gi in range(PARAM_BATCH):
        _ssm_param_group(
            gi % 2, ldt_ref.at[gi], lre_ref.at[gi], lim_ref.at[gi], lrec_ref.at[gi], limc_ref.at[gi],
            btre_ref.at[gi], btim_ref.at[gi], c4re_ref.at[gi], c4im_ref.at[gi], bpre_ref.at[gi],
            bpim_ref.at[gi], cre_ref.at[gi], cim_ref.at[gi],
            tt_ref.at[gi], w_ref.at[gi // 2, pl.ds((gi % 2) * CW, CW)], vt_ref.at[gi])


def _ssm_param_group(h, ldt_ref, lre_ref, lim_ref, lrec_ref, limc_ref, btre_ref, btim_ref,
                     c4re_ref, c4im_ref, bpre_ref, bpim_ref, cre_ref, cim_ref, tt_ref, w_ref, vt_ref):
    dt = jnp.exp(ldt_ref[...])

    def cis_pow(n, lr_, li_):
        mag = jnp.exp(n * (lr_ * dt))
        ang = n * (li_ * dt)
        return mag * jnp.cos(ang), mag * jnp.sin(ang)

    def zoh(a_re, a_im, lr_, li_):
        nr, ni = a_re - 1.0, a_im
        den = lr_ * lr_ + li_ * li_
        return (nr * lr_ + ni * li_) / den, (ni * lr_ - nr * li_) / den

    def complex_mul(x_re, x_im, y_re, y_im):
        return x_re * y_re - x_im * y_im, x_re * y_im + x_im * y_re

    lr, li = lre_ref[...], lim_ref[...]
    n_rows = jnp.minimum(lax.broadcasted_iota(jnp.int32, (24, CW), 0), CHUNK).astype(F32)
    p_re, p_im = cis_pow(n_rows, lr, li)
    coef_re, coef_im = zoh(p_re[1:2], p_im[1:2], lr, li)
    bb_re, bb_im = complex_mul(coef_re, coef_im, btre_ref[...], btim_ref[...])
    c4_re, c4_im = c4re_ref[...], c4im_ref[...]
    slot = lax.broadcasted_iota(jnp.int32, (GROUP, CW), 1) // STATE

    def to_slots(v_re, v_im):
        return jnp.where(slot == h, v_re, jnp.where(slot == h + 2, v_im, 0.0)).astype(BF16)

    for s in range(CHUNK):
        n = CHUNK - 1 - s
        w_re, w_im = complex_mul(bb_re, bb_im, p_re[n:n + 1], p_im[n:n + 1])
        w_ref[s * GROUP:(s + 1) * GROUP, :] = to_slots(w_re, w_im)
        v_re, v_im = complex_mul(c4_re, c4_im, p_re[s + 1:s + 2], p_im[s + 1:s + 2])
        vt_ref[s * GROUP:(s + 1) * GROUP, :] = to_slots(v_re, -v_im)

    lrc, lic = lrec_ref[...], limc_ref[...]
    n_lanes = jnp.minimum(lax.broadcasted_iota(jnp.int32, (STATE, 128), 1), CHUNK).astype(F32)
    q_re, q_im = cis_pow(n_lanes, lrc, lic)
    cc_re, cc_im = zoh(q_re[:, 1:2], q_im[:, 1:2], lrc, lic)
    bp_re, bp_im = complex_mul(cc_re, cc_im, bpre_ref[...], bpim_ref[...])
    sel = lax.broadcasted_iota(jnp.int32, (128, CW), 0)
    lane_r = lax.broadcasted_iota(jnp.int32, (128, CW), 1) // GROUP
    lane_ci = lax.broadcasted_iota(jnp.int32, (128, CW), 1) % GROUP
    e_pow = (sel == CHUNK - 1 - lane_r).astype(F32)
    e_ci = (sel == lane_ci).astype(F32)

    def hdot(a, b):
        return jnp.dot(a, b, preferred_element_type=F32, precision=HIGHEST)

    r_re, r_im = complex_mul(hdot(q_re, e_pow), hdot(q_im, e_pow), hdot(bp_re, e_ci), hdot(bp_im, e_ci))
    krev = hdot(cre_ref[...], r_re) - hdot(cim_ref[...], r_im)
    lane_s = lax.broadcasted_iota(jnp.int32, (GROUP, CW), 1) // GROUP
    for t in range(CHUNK):
        shift = (CHUNK - 1 - t) * GROUP
        moved = krev if shift == 0 else pltpu.roll(krev, CW - shift, axis=1)
        tt_ref[t * GROUP:(t + 1) * GROUP, :] = jnp.where(lane_s <= t, moved, 0.0).astype(BF16)


def _ssm_params(log_dt, lam_re, lam_im, b_re, b_im, c_re, c_im):
    tile4 = lambda a: jnp.tile(a, (1, 1, 4))
    pad128 = lambda a: jnp.pad(a, ((0, 0), (0, 0), (0, 128 - GROUP)))
    ldt = log_dt.reshape(GROUPS, 1, 1)
    lre4 = tile4(lam_re.reshape(GROUPS, 1, STATE))
    lim4 = tile4(lam_im.reshape(GROUPS, 1, STATE))
    lrec = lam_re.reshape(GROUPS, STATE, 1)
    limc = lam_im.reshape(GROUPS, STATE, 1)
    bt_re = tile4(jnp.swapaxes(b_re, 1, 2))
    bt_im = tile4(jnp.swapaxes(b_im, 1, 2))
    c4_re, c4_im = tile4(c_re), tile4(c_im)
    bp_re, bp_im = pad128(b_re), pad128(b_im)

    def spec(shape):
        return pl.BlockSpec((PARAM_BATCH,) + shape, lambda g: (g, 0, 0))

    return pl.pallas_call(
        _ssm_param_kernel,
        grid=(GROUPS // PARAM_BATCH,),
        in_specs=[spec((1, 1)), spec((1, CW)), spec((1, CW)), spec((STATE, 1)), spec((STATE, 1)),
                  spec((GROUP, CW)), spec((GROUP, CW)), spec((GROUP, CW)), spec((GROUP, CW)),
                  spec((STATE, 128)), spec((STATE, 128)), spec((GROUP, STATE)), spec((GROUP, STATE))],
        out_specs=[
            pl.BlockSpec((PARAM_BATCH, CW, CW), lambda g: (g, 0, 0)),
            pl.BlockSpec((PARAM_BATCH // 2, 2 * CW, CW), lambda g: (g, 0, 0)),
            pl.BlockSpec((PARAM_BATCH, CW, CW), lambda g: (g, 0, 0)),
        ],
        out_shape=[
            jax.ShapeDtypeStruct((GROUPS, CW, CW), BF16),
            jax.ShapeDtypeStruct((PAIRS, 2 * CW, CW), BF16),
            jax.ShapeDtypeStruct((GROUPS, CW, CW), BF16),
        ],
        compiler_params=pltpu.CompilerParams(dimension_semantics=("arbitrary",)),
        name="ssm_params",
    )(ldt, lre4, lim4, lrec, limc, bt_re, bt_im, c4_re, c4_im, bp_re, bp_im, c_re, c_im)


def _ssm_kernel(u_ref, tt_ref, w_ref, vt_ref, d_ref, are_ref, aim_ref, y_ref,
                uf_ref, lt_ref, sre_ref, sim_ref, yt_ref, st_ref, xr_ref, xi_ref, *, bsz, ct):
    cols = bsz * ct
    halves = [slice(0, 128), slice(128, 256)]

    @pl.when(pl.program_id(1) == 0)
    def _():
        xr_ref[...] = jnp.zeros(xr_ref.shape, F32)
        xi_ref[...] = jnp.zeros(xi_ref.shape, F32)

    for b in range(bsz):
        u = u_ref[b].astype(F32)
        for hf, sl in enumerate(halves):
            uf_ref[b, hf] = u[:, sl]
    for s in range(CHUNK):
        x_s = jnp.concatenate(
            [jnp.concatenate([uf_ref[b, hf, pl.ds(s, ct, stride=CHUNK), :] for b in range(bsz)], axis=0)
             for hf in range(2)], axis=1)
        lt_ref[:, s * GROUP:(s + 1) * GROUP, :] = x_s.T.astype(BF16).reshape(SG, GROUP, cols)

    def pair_rows(gp):
        return slice(gp * cols, (gp + 1) * cols)

    for gp in range(SP):
        lt_pair = lt_ref[2 * gp:2 * gp + 2].reshape(2 * CW, cols)
        s = lax.dot_general(lt_pair, w_ref[gp], (((0,), (0,)), ((), ())), preferred_element_type=F32)
        sre_ref[pair_rows(gp), :] = s[:, :128]
        sim_ref[pair_rows(gp), :] = s[:, 128:]

    a_re, a_im = are_ref[...], aim_ref[...]

    def scan_step(n, c):
        rows = pl.ds(n, SP * bsz, stride=ct)
        s_re, s_im = sre_ref[rows, :], sim_ref[rows, :]
        x_re, x_im = xr_ref[...], xi_ref[...]
        sre_ref[rows, :] = x_re
        sim_ref[rows, :] = x_im
        xr_ref[...] = a_re * x_re - a_im * x_im + s_re
        xi_ref[...] = a_re * x_im + a_im * x_re + s_im
        return c

    lax.fori_loop(0, ct, scan_step, 0, unroll=4)

    for gp in range(SP):
        x_in = jnp.concatenate([sre_ref[pair_rows(gp), :], sim_ref[pair_rows(gp), :]],
                               axis=1).astype(BF16)
        for hh in range(2):
            g = 2 * gp + hh
            yt_ref[g] = _dot(tt_ref[g], lt_ref[g]) + lax.dot_general(
                vt_ref[g], x_in, (((1,), (1,)), ((), ())), preferred_element_type=F32)

    for t in range(CHUNK):
        z = yt_ref[:, t * GROUP:(t + 1) * GROUP, :].reshape(SG * GROUP, cols).T
        for b in range(bsz):
            for hf, sl in enumerate(halves):
                st_ref[b, hf, pl.ds(t, ct, stride=CHUNK), :] = z[b * ct:(b + 1) * ct, sl]
    for b in range(bsz):
        pre = jnp.concatenate([st_ref[b, 0], st_ref[b, 1]], axis=1)
        u = jnp.concatenate([uf_ref[b, 0], uf_ref[b, 1]], axis=1)
        y_ref[b] = jax.nn.gelu(pre + d_ref[...] * u).astype(BF16)


def _ssm(proj3, tt_mat, w_mat, vt_mat, d_skip, a_re, a_im, bsz, seq, ct):
    tok = ct * CHUNK
    cols = bsz * ct
    kern = functools.partial(_ssm_kernel, bsz=bsz, ct=ct)
    ucol = TILE_U * SSM_SPLIT
    return pl.pallas_call(
        kern,
        grid=(SSM_SPLIT, seq // tok),
        in_specs=[
            pl.BlockSpec((bsz, tok, CW), lambda q, t: (0, t, ucol + q)),
            pl.BlockSpec((SG, CW, CW), lambda q, t: (q, 0, 0)),
            pl.BlockSpec((SP, 2 * CW, CW), lambda q, t: (q, 0, 0)),
            pl.BlockSpec((SG, CW, CW), lambda q, t: (q, 0, 0)),
            pl.BlockSpec((1, CW), lambda q, t: (0, q)),
            pl.BlockSpec((None, SP * bsz, 128), lambda q, t: (q, 0, 0)),
            pl.BlockSpec((None, SP * bsz, 128), lambda q, t: (q, 0, 0)),
        ],
        out_specs=pl.BlockSpec((bsz, tok, CW), lambda q, t: (0, t, q)),
        out_shape=jax.ShapeDtypeStruct((bsz, seq, D_SSM), BF16),
        scratch_shapes=[
            pltpu.VMEM((bsz, 2, tok, 128), F32),
            pltpu.VMEM((SG, CW, cols), BF16),
            pltpu.VMEM((SP * cols, 128), F32),
            pltpu.VMEM((SP * cols, 128), F32),
            pltpu.VMEM((SG, CW, cols), F32),
            pltpu.VMEM((bsz, 2, tok, 128), F32),
            pltpu.VMEM((SP * bsz, 128), F32),
            pltpu.VMEM((SP * bsz, 128), F32),
        ],
        compiler_params=pltpu.CompilerParams(
            dimension_semantics=("arbitrary", "arbitrary"), vmem_limit_bytes=VMEM_LIMIT),
        name="ssm_scan",
    )(proj3, tt_mat, w_mat, vt_mat, d_skip, a_re, a_im)


def _chunk_decay_kernel(ldt_ref, lre_ref, lim_ref, are_ref, aim_ref):
    dt = jnp.exp(ldt_ref[...])
    mag = jnp.exp(float(CHUNK) * (lre_ref[...] * dt))
    ang = float(CHUNK) * (lim_ref[...] * dt)
    are_ref[...] = mag * jnp.cos(ang)
    aim_ref[...] = mag * jnp.sin(ang)


def _chunk_decay(ldt_p, lre_p, lim_p):
    shape = jax.ShapeDtypeStruct(ldt_p.shape, F32)
    return pl.pallas_call(_chunk_decay_kernel, out_shape=[shape, shape], name="ssm_chunk_decay")(
        ldt_p, lre_p, lim_p)


def _merge_kernel(x_ref, ya_ref, yg_ref, zs_ref, ga_ref, gs_ref, wg_ref, bg_ref,
                  wpa_ref, wps_ref, wo_ref, o_ref):
    yg = yg_ref[...]
    glu = _dot(yg, wg_ref[...]) + bg_ref[...]
    y_ssm = (yg.astype(F32) * jax.nn.sigmoid(glu) * zs_ref[...].astype(F32)).astype(BF16)
    merged = (ga_ref[...].astype(F32) * _dot(ya_ref[...], wpa_ref[...])
              + gs_ref[...].astype(F32) * _dot(y_ssm, wps_ref[...]))
    o_ref[...] = x_ref[...] + _dot(merged.astype(BF16), wo_ref[...])


def _merge(x2, y_att, yg, proj, w_glu, b_glu, w_pa, w_ps, w_out, tm):
    tokens = x2.shape[0]
    const = lambda i: (0, 0)
    once = pl.Buffered(1)
    return pl.pallas_call(
        _merge_kernel,
        grid=(tokens // tm,),
        in_specs=[
            pl.BlockSpec((tm, D_MODEL), lambda i: (i, 0)),
            pl.BlockSpec((tm, D_ATT), lambda i: (i, 0)),
            pl.BlockSpec((tm, D_SSM), lambda i: (i, 0)),
            pl.BlockSpec((tm, D_SSM), lambda i: (i, TILE_ZSSM)),
            pl.BlockSpec((tm, D_MODEL), lambda i: (i, TILE_GATT0 // 2)),
            pl.BlockSpec((tm, D_MODEL), lambda i: (i, TILE_GSSM0 // 2)),
            pl.BlockSpec((D_SSM, D_SSM), const, pipeline_mode=once),
            pl.BlockSpec((1, D_SSM), const, pipeline_mode=once),
            pl.BlockSpec((D_ATT, D_MODEL), const, pipeline_mode=once),
            pl.BlockSpec((D_SSM, D_MODEL), const, pipeline_mode=once),
            pl.BlockSpec((D_MODEL, D_MODEL), const, pipeline_mode=once),
        ],
        out_specs=pl.BlockSpec((tm, D_MODEL), lambda i: (i, 0)),
        out_shape=jax.ShapeDtypeStruct((tokens, D_MODEL), F32),
        compiler_params=pltpu.CompilerParams(
            dimension_semantics=("arbitrary",), vmem_limit_bytes=VMEM_LIMIT),
        name="merge_out",
    )(x2, y_att, yg, proj, proj, proj, w_glu, b_glu, w_pa, w_ps, w_out)


def kernel(x, ln_gain, w_in, q_norm_gain, k_norm_gain, lambda_q1, lambda_k1, lambda_q2, lambda_k2,
           subln_gain, ssm_lambda_re, ssm_lambda_im, ssm_log_dt, ssm_b_re, ssm_b_im, ssm_c_re,
           ssm_c_im, ssm_d, w_glu, b_glu, w_proj_att, w_proj_ssm, w_out):
    bsz, seq, _ = x.shape
    assert ln_gain.shape[0] == 1 and x.shape[2] == D_MODEL and w_in.shape[2] == N_IN
    tokens = bsz * seq
    tm_in = min(1024, seq)
    tq = min(1024, seq)
    tm_out = min(512, seq)
    ct = 128 // bsz
    assert 128 % bsz == 0 and ct % 8 == 0 and seq % (ct * CHUNK) == 0

    x2 = x.reshape(tokens, D_MODEL)
    scale = DQK ** -0.5
    q_gain = (jnp.tile(q_norm_gain[0], 2 * HEADS) * scale).reshape(1, 1024)
    k_gain = jnp.tile(k_norm_gain[0], 2 * HEADS).reshape(1, 1024)
    proj, vt = _in_proj(x2, ln_gain[0].reshape(1, D_MODEL), w_in[0].astype(BF16), q_gain, k_gain,
                        bsz, seq, tm_in)
    proj3 = proj.reshape(bsz, seq, N_IN)

    lam_params = jnp.stack([lambda_q1[0], lambda_k1[0], lambda_q2[0], lambda_k2[0]])
    subln = (subln_gain[0] * (1.0 - LAMBDA_INIT)).reshape(1, DV)
    score_bound = 1.02 * 8.0 * jnp.max(jnp.abs(q_norm_gain[0] * k_norm_gain[0]))
    attend = functools.partial(_attention, bsz=bsz, seq=seq, tq=tq)
    y_att = lax.cond(score_bound <= SCORE_BOUND_FAST,
                     functools.partial(attend, fast=True), functools.partial(attend, fast=False),
                     lam_params, subln, proj3, vt)

    tt_mat, w_mat, vt_mat = _ssm_params(ssm_log_dt[0], ssm_lambda_re[0], ssm_lambda_im[0],
                                        ssm_b_re[0], ssm_b_im[0], ssm_c_re[0], ssm_c_im[0])
    per_row = lambda a: jnp.repeat(a.reshape(PAIRS, 128), bsz, axis=0).reshape(SSM_SPLIT, SP * bsz, 128)
    a_re, a_im = _chunk_decay(per_row(jnp.repeat(ssm_log_dt[0], STATE)), per_row(ssm_lambda_re[0]),
                              per_row(ssm_lambda_im[0]))
    yg = _ssm(proj3, tt_mat, w_mat, vt_mat, ssm_d[0].reshape(1, D_SSM), a_re, a_im, bsz, seq, ct)

    out = _merge(x2, y_att.reshape(tokens, D_ATT), yg.reshape(tokens, D_SSM), proj, w_glu[0].astype(BF16),
                 b_glu[0].reshape(1, D_SSM), w_proj_att[0].astype(BF16), w_proj_ssm[0].astype(BF16),
                 w_out[0].astype(BF16), tm_out)
    return out.reshape(bsz, seq, D_MODEL)
```
